```python
import math
import jax, jax.numpy as jnp
from jax import lax
import numpy as np

D_MODEL = 2048
BATCH = 8
SEQ = 8192
DEPTH = 2

N_MIXERS = 2
N_A_LAYERS = (DEPTH + 1) // 2
N_B_LAYERS = DEPTH // 2
EPS = 1e-6

M_HEADS = 8
M_QK_DIM = D_MODEL // 2
M_V_DIM = D_MODEL
M_DK = M_QK_DIM // M_HEADS
M_DV = M_V_DIM // M_HEADS
M_PROJ = 2 * M_QK_DIM + 2 * M_V_DIM + 2 * M_HEADS
M_CHUNK = 64

R_WIDTH = D_MODEL
R_BLOCKS = 8
R_BLOCK_W = R_WIDTH // R_BLOCKS
R_CONV_W = 4
R_C = 8.0

D_FF = ((8 * D_MODEL // 3 + 255) // 256) * 256

kernel_name = "hybrid_mlstm_rglru_interleaved"


def rms_norm(x, g):
    xf = x.astype(jnp.float32)
    y = xf * lax.rsqrt(jnp.mean(xf * xf, axis=-1, keepdims=True) + EPS)
    return (y * g.astype(jnp.float32)).astype(x.dtype)


def swiglu(x, w_in, w_out):
    g, u = jnp.split(x @ w_in, 2, axis=-1)
    return (jax.nn.silu(g) * u) @ w_out


def mlstm_chunkwise(q, k, v, ig, lf):
    B, H, S, DK = q.shape
    DV = v.shape[-1]
    nc = S // M_CHUNK

    def to_chunks(t):
        return jnp.moveaxis(t.reshape(B, H, nc, M_CHUNK, *t.shape[3:]), 2, 0)

    qc, kc, vc, ic, fc = (to_chunks(t) for t in (q, k, v, ig, lf))
    causal = jnp.tril(jnp.ones((M_CHUNK, M_CHUNK), dtype=bool))

    def step(carry, xs):
        C, n, m = carry
        qb, kb, vb, ib, fb = xs
        b = jnp.cumsum(fb, axis=-1)
        dmat = b[..., :, None] - b[..., None, :] + ib[..., None, :]
        dmat = jnp.where(causal, dmat, -jnp.inf)
        m_inter = b + m[..., None]
        m_t = jnp.maximum(m_inter, jnp.max(dmat, axis=-1))
        w = jnp.exp(dmat - m_t[..., None])
        s = jnp.einsum('bhtd,bhsd->bhts', qb, kb) * w
        scale_inter = jnp.exp(m_inter - m_t)
        num = (jnp.einsum('bhts,bhsv->bhtv', s, vb)
               + scale_inter[..., None] * jnp.einsum('bhtd,bhdv->bhtv', qb, C))
        den = jnp.sum(s, axis=-1) + scale_inter * jnp.einsum('bhtd,bhd->bht', qb, n)
        h = num / jnp.maximum(jnp.abs(den), jnp.exp(-m_t))[..., None]
        b_last = b[..., -1]
        g = b_last[..., None] - b + ib
        m_new = jnp.maximum(b_last + m, jnp.max(g, axis=-1))
        wk = jnp.exp(g - m_new[..., None])
        decay = jnp.exp(b_last + m - m_new)
        kw = kb * wk[..., None]
        C_new = decay[..., None, None] * C + jnp.einsum('bhsd,bhsv->bhdv', kw, vb)
        n_new = decay[..., None] * n + jnp.sum(kw, axis=-2)
        return (C_new, n_new, m_new), h

    init = (jnp.zeros((B, H, DK, DV), jnp.float32),
            jnp.zeros((B, H, DK), jnp.float32),
            jnp.zeros((B, H), jnp.float32))
    _, hc = lax.scan(step, init, (qc, kc, vc, ic, fc))
    return jnp.moveaxis(hc, 0, 2).reshape(B, H, S, DV)


def mlstm_mixer(x, w_in, b_if, head_norm, w_out):
    B, S, _ = x.shape
    proj = x @ w_in
    q, k, v, o, if_pre = jnp.split(
        proj, [M_QK_DIM, 2 * M_QK_DIM, 2 * M_QK_DIM + M_V_DIM, 2 * M_QK_DIM + 2 * M_V_DIM], axis=-1)

    def heads(t, d):
        return t.reshape(B, S, M_HEADS, d).transpose(0, 2, 1, 3).astype(jnp.float32)

    qh = heads(q, M_DK)
    kh = heads(k, M_DK) * (M_DK ** -0.5)
    vh = heads(v, M_DV)
    gates = (if_pre + b_if).astype(jnp.float32).reshape(B, S, 2, M_HEADS)
    ig = gates[:, :, 0].transpose(0, 2, 1)
    lf = jax.nn.log_sigmoid(gates[:, :, 1]).transpose(0, 2, 1)
    h = mlstm_chunkwise(qh, kh, vh, ig, lf)
    h = h * lax.rsqrt(jnp.mean(h * h, axis=-1, keepdims=True) + EPS)
    h = h.transpose(0, 2, 1, 3).reshape(B, S, M_V_DIM) * head_norm.astype(jnp.float32)
    h = h * jax.nn.sigmoid(o.astype(jnp.float32))
    return h.astype(x.dtype) @ w_out


def causal_depthwise_conv(x, w, b):
    C = x.shape[-1]
    y = lax.conv_general_dilated(
        x, w[:, None, :].astype(x.dtype), window_strides=(1,),
        padding=[(R_CONV_W - 1, 0)], dimension_numbers=('NWC', 'WIO', 'NWC'),
        feature_group_count=C)
    return y + b


def rglru_mixer(x, w_in, conv_w, conv_b, gate_w, gate_b, a_param, w_out):
    B, S, _ = x.shape
    gate_branch, rec = jnp.split(x @ w_in, 2, axis=-1)
    rec = causal_depthwise_conv(rec, conv_w, conv_b)
    xb = rec.reshape(B, S, R_BLOCKS, R_BLOCK_W)
    gates = jnp.einsum('bsgi,gio->bsgo', xb, gate_w) + gate_b
    r_pre, i_pre = jnp.split(gates.astype(jnp.float32), 2, axis=-1)
    r = jax.nn.sigmoid(r_pre).reshape(B, S, R_WIDTH)
    i = jax.nn.sigmoid(i_pre).reshape(B, S, R_WIDTH)
    log_a = R_C * r * jax.nn.log_sigmoid(a_param.astype(jnp.float32))
    a = jnp.exp(log_a)
    mult = jnp.sqrt(-jnp.expm1(2.0 * log_a))
    u = mult * (i * rec.astype(jnp.float32))

    def combine(left, right):
        a1, b1 = left
        a2, b2 = right
        return a2 * a1, a2 * b1 + b2

    _, h = lax.associative_scan(combine, (a, u), axis=1)
    y = jax.nn.gelu(gate_branch.astype(jnp.float32)) * h
    return y.astype(x.dtype) @ w_out


def _fwd_setup_inputs(seed: int = 0) -> dict:
    key = jax.random.key(seed)
    ks = jax.random.split(key, 20)
    f32 = jnp.float32

    def nrm(k, shape, scale):
        return jax.random.normal(k, shape, f32) * scale

    x = jax.random.normal(ks[0], (BATCH, SEQ, D_MODEL), f32)
    norm_mix = 1.0 + nrm(ks[1], (DEPTH, D_MODEL), 0.02)
    norm_ffn = 1.0 + nrm(ks[2], (DEPTH, D_MODEL), 0.02)
    norm_final = 1.0 + nrm(ks[3], (D_MODEL,), 0.02)

    m_w_in = nrm(ks[4], (N_A_LAYERS, D_MODEL, M_PROJ), D_MODEL ** -0.5)
    kb1, kb2 = jax.random.split(ks[5])
    m_b_if = jnp.concatenate([nrm(kb1, (N_A_LAYERS, M_HEADS), 0.1),
                              3.0 + nrm(kb2, (N_A_LAYERS, M_HEADS), 0.1)], axis=-1)
    m_head_norm = 1.0 + nrm(ks[6], (N_A_LAYERS, M_V_DIM), 0.02)
    m_w_out = nrm(ks[7], (N_A_LAYERS, M_V_DIM, D_MODEL), M_V_DIM ** -0.5)

    r_w_in = nrm(ks[8], (N_B_LAYERS, D_MODEL, 2 * R_WIDTH), D_MODEL ** -0.5)
    r_conv_w = nrm(ks[9], (N_B_LAYERS, R_CONV_W, R_WIDTH), R_CONV_W ** -0.5)
    r_conv_b = nrm(ks[10], (N_B_LAYERS, R_WIDTH), 0.02)
    r_gate_w = nrm(ks[11], (N_B_LAYERS, R_BLOCKS, R_BLOCK_W, 2 * R_BLOCK_W), R_BLOCK_W ** -0.5)
    r_gate_b = nrm(ks[12], (N_B_LAYERS, R_BLOCKS, 2 * R_BLOCK_W), 0.02)
    a_c = jax.random.uniform(ks[13], (N_B_LAYERS, R_WIDTH), f32, 0.9, 0.999)
    a_base = a_c ** (1.0 / R_C)
    r_a_param = jnp.log(a_base) - jnp.log1p(-a_base)
    r_w_out = nrm(ks[14], (N_B_LAYERS, R_WIDTH, D_MODEL), R_WIDTH ** -0.5)

    ffn_w_in = nrm(ks[15], (DEPTH, D_MODEL, 2 * D_FF), D_MODEL ** -0.5)
    ffn_w_out = nrm(ks[16], (DEPTH, D_FF, D_MODEL), D_FF ** -0.5)

    return {"x": x, "norm_mix": norm_mix, "norm_ffn": norm_ffn, "norm_final": norm_final,
            "m_w_in": m_w_in, "m_b_if": m_b_if, "m_head_norm": m_head_norm, "m_w_out": m_w_out,
            "r_w_in": r_w_in, "r_conv_w": r_conv_w, "r_conv_b": r_conv_b, "r_gate_w": r_gate_w,
            "r_gate_b": r_gate_b, "r_a_param": r_a_param, "r_w_out": r_w_out,
            "ffn_w_in": ffn_w_in, "ffn_w_out": ffn_w_out}


def _fwd_reference(x, norm_mix, norm_ffn, norm_final, m_w_in, m_b_if, m_head_norm, m_w_out,
              r_w_in, r_conv_w, r_conv_b, r_gate_w, r_gate_b, r_a_param, r_w_out,
              ffn_w_in, ffn_w_out):
    h = x
    for layer in range(DEPTH):
        hn = rms_norm(h, norm_mix[layer])
        j = layer // N_MIXERS
        if layer % N_MIXERS == 0:
            mix = mlstm_mixer(hn, m_w_in[j], m_b_if[j], m_head_norm[j], m_w_out[j])
        else:
            mix = rglru_mixer(hn, r_w_in[j], r_conv_w[j], r_conv_b[j], r_gate_w[j],
                              r_gate_b[j], r_a_param[j], r_w_out[j])
        h = h + mix
        h = h + swiglu(rms_norm(h, norm_ffn[layer]), ffn_w_in[layer], ffn_w_out[layer])
    return rms_norm(h, norm_final)


import jax as _jax
import jax.numpy as _jnp

TWIN_FORMAT = 'train_step'
FWD_PARAMS = ['x', 'norm_mix', 'norm_ffn', 'norm_final', 'm_w_in', 'm_b_if', 'm_head_norm', 'm_w_out', 'r_w_in', 'r_conv_w', 'r_conv_b', 'r_gate_w', 'r_gate_b', 'r_a_param', 'r_w_out', 'ffn_w_in', 'ffn_w_out']
TWIN_WEIGHTS = ['norm_mix', 'norm_ffn', 'norm_final', 'm_w_in', 'm_b_if', 'm_head_norm', 'm_w_out', 'r_w_in', 'r_conv_w', 'r_conv_b', 'r_gate_w', 'r_gate_b', 'r_a_param', 'r_w_out', 'ffn_w_in', 'ffn_w_out']
TWIN_DIFF_INPUT = 'x'
TWIN_INPUTS = ['x', 'norm_mix', 'norm_ffn', 'norm_final', 'm_w_in', 'm_b_if', 'm_head_norm', 'm_w_out', 'r_w_in', 'r_conv_w', 'r_conv_b', 'r_gate_w', 'r_gate_b', 'r_a_param', 'r_w_out', 'ffn_w_in', 'ffn_w_out', 'loss_target', 'm_norm_mix', 'm_norm_ffn', 'm_norm_final', 'm_m_w_in', 'm_m_b_if', 'm_m_head_norm', 'm_m_w_out', 'm_r_w_in', 'm_r_conv_w', 'm_r_conv_b', 'm_r_gate_w', 'm_r_gate_b', 'm_r_a_param', 'm_r_w_out', 'm_ffn_w_in', 'm_ffn_w_out', 'v_norm_mix', 'v_norm_ffn', 'v_norm_final', 'v_m_w_in', 'v_m_b_if', 'v_m_head_norm', 'v_m_w_out', 'v_r_w_in', 'v_r_conv_w', 'v_r_conv_b', 'v_r_gate_w', 'v_r_gate_b', 'v_r_a_param', 'v_r_w_out', 'v_ffn_w_in', 'v_ffn_w_out']
TWIN_OUTPUTS = ['loss', 'grad_x', 'grad_norm_mix', 'grad_norm_ffn', 'grad_norm_final', 'grad_m_w_in', 'grad_m_b_if', 'grad_m_head_norm', 'grad_m_w_out', 'grad_r_w_in', 'grad_r_conv_w', 'grad_r_conv_b', 'grad_r_gate_w', 'grad_r_gate_b', 'grad_r_a_param', 'grad_r_w_out', 'grad_ffn_w_in', 'grad_ffn_w_out', 'delta_norm_mix', 'delta_norm_ffn', 'delta_norm_final', 'delta_m_w_in', 'delta_m_b_if', 'delta_m_head_norm', 'delta_m_w_out', 'delta_r_w_in', 'delta_r_conv_w', 'delta_r_conv_b', 'delta_r_gate_w', 'delta_r_gate_b', 'delta_r_a_param', 'delta_r_w_out', 'delta_ffn_w_in', 'delta_ffn_w_out', 'new_m_norm_mix', 'new_m_norm_ffn', 'new_m_norm_final', 'new_m_m_w_in', 'new_m_m_b_if', 'new_m_m_head_norm', 'new_m_m_w_out', 'new_m_r_w_in', 'new_m_r_conv_w', 'new_m_r_conv_b', 'new_m_r_gate_w', 'new_m_r_gate_b', 'new_m_r_a_param', 'new_m_r_w_out', 'new_m_ffn_w_in', 'new_m_ffn_w_out', 'new_v_norm_mix', 'new_v_norm_ffn', 'new_v_norm_final', 'new_v_m_w_in', 'new_v_m_b_if', 'new_v_m_head_norm', 'new_v_m_w_out', 'new_v_r_w_in', 'new_v_r_conv_w', 'new_v_r_conv_b', 'new_v_r_gate_w', 'new_v_r_gate_b', 'new_v_r_a_param', 'new_v_r_w_out', 'new_v_ffn_w_in', 'new_v_ffn_w_out']
TWIN_LEAF_KINDS = {'loss': 'loss', 'grad_x': 'grad_x', 'grad_norm_mix': 'grad_w', 'grad_norm_ffn': 'grad_w', 'grad_norm_final': 'grad_w', 'grad_m_w_in': 'grad_w', 'grad_m_b_if': 'grad_w', 'grad_m_head_norm': 'grad_w', 'grad_m_w_out': 'grad_w', 'grad_r_w_in': 'grad_w', 'grad_r_conv_w': 'grad_w', 'grad_r_conv_b': 'grad_w', 'grad_r_gate_w': 'grad_w', 'grad_r_gate_b': 'grad_w', 'grad_r_a_param': 'grad_w', 'grad_r_w_out': 'grad_w', 'grad_ffn_w_in': 'grad_w', 'grad_ffn_w_out': 'grad_w', 'delta_norm_mix': 'delta_w', 'delta_norm_ffn': 'delta_w', 'delta_norm_final': 'delta_w', 'delta_m_w_in': 'delta_w', 'delta_m_b_if': 'delta_w', 'delta_m_head_norm': 'delta_w', 'delta_m_w_out': 'delta_w', 'delta_r_w_in': 'delta_w', 'delta_r_conv_w': 'delta_w', 'delta_r_conv_b': 'delta_w', 'delta_r_gate_w': 'delta_w', 'delta_r_gate_b': 'delta_w', 'delta_r_a_param': 'delta_w', 'delta_r_w_out': 'delta_w', 'delta_ffn_w_in': 'delta_w', 'delta_ffn_w_out': 'delta_w', 'new_m_norm_mix': 'new_m', 'new_m_norm_ffn': 'new_m', 'new_m_norm_final': 'new_m', 'new_m_m_w_in': 'new_m', 'new_m_m_b_if': 'new_m', 'new_m_m_head_norm': 'new_m', 'new_m_m_w_out': 'new_m', 'new_m_r_w_in': 'new_m', 'new_m_r_conv_w': 'new_m', 'new_m_r_conv_b': 'new_m', 'new_m_r_gate_w': 'new_m', 'new_m_r_gate_b': 'new_m', 'new_m_r_a_param': 'new_m', 'new_m_r_w_out': 'new_m', 'new_m_ffn_w_in': 'new_m', 'new_m_ffn_w_out': 'new_m', 'new_v_norm_mix': 'new_v', 'new_v_norm_ffn': 'new_v', 'new_v_norm_final': 'new_v', 'new_v_m_w_in': 'new_v', 'new_v_m_b_if': 'new_v', 'new_v_m_head_norm': 'new_v', 'new_v_m_w_out': 'new_v', 'new_v_r_w_in': 'new_v', 'new_v_r_conv_w': 'new_v', 'new_v_r_conv_b': 'new_v', 'new_v_r_gate_w': 'new_v', 'new_v_r_gate_b': 'new_v', 'new_v_r_a_param': 'new_v', 'new_v_r_w_out': 'new_v', 'new_v_ffn_w_in': 'new_v', 'new_v_ffn_w_out': 'new_v'}


def _forward(args):
    return _fwd_reference(*[args[k] for k in FWD_PARAMS])


def _output_shape():
    def fwd():
        inp = _fwd_setup_inputs(0)
        return _fwd_reference(*[inp[k] for k in FWD_PARAMS])
    out = _jax.eval_shape(fwd)
    return out.shape, out.dtype

N_MICROBATCH = 1
ADAM_LR = 0.001
ADAM_B1 = 0.9
ADAM_B2 = 0.999
ADAM_EPS = 1e-08
ADAM_WD = 0.01
ADAM_STEP = 10
PER_EXAMPLE_BATCH_AXIS = {'x': 0, 'loss_target': 0}
SHARED_INPUTS = []
_WEIGHT_DTYPES = {'norm_mix': _jnp.float32, 'norm_ffn': _jnp.float32, 'norm_final': _jnp.float32, 'm_w_in': _jnp.float32, 'm_b_if': _jnp.float32, 'm_head_norm': _jnp.float32, 'm_w_out': _jnp.float32, 'r_w_in': _jnp.float32, 'r_conv_w': _jnp.float32, 'r_conv_b': _jnp.float32, 'r_gate_w': _jnp.float32, 'r_gate_b': _jnp.float32, 'r_a_param': _jnp.float32, 'r_w_out': _jnp.float32, 'ffn_w_in': _jnp.float32, 'ffn_w_out': _jnp.float32}
MOMENT_SCALE = {'norm_mix': 1.271213e-01, 'norm_ffn': 8.444646e-02, 'norm_final': 3.194352e+01, 'm_w_in': 9.296865e-02, 'm_b_if': 6.955238e-01, 'm_head_norm': 7.718084e-02, 'm_w_out': 7.514279e-02, 'r_w_in': 4.320458e-02, 'r_conv_w': 4.584068e-02, 'r_conv_b': 5.393576e-01, 'r_gate_w': 1.651933e-02, 'r_gate_b': 1.385650e-02, 'r_a_param': 2.143091e-02, 'r_w_out': 4.334606e-02, 'ffn_w_in': 3.609300e-02, 'ffn_w_out': 5.899932e-02}


def _to_microbatches(a, axis):
    t = _jnp.moveaxis(a, axis, 0)
    t = t.reshape((N_MICROBATCH, t.shape[0] // N_MICROBATCH) + t.shape[1:])
    return _jnp.moveaxis(t, 1, axis + 1)


def setup_inputs(seed: int = 0) -> dict:
    inp = _fwd_setup_inputs(seed)
    key = _jax.random.fold_in(_jax.random.key(seed), 7919)
    shape, _ = _output_shape()
    out = dict(inp)
    out["loss_target"] = _jax.random.normal(_jax.random.fold_in(key, 0), shape, _jnp.float32)
    for i, name in enumerate(TWIN_WEIGHTS):
        w = inp[name].astype(_jnp.float32)
        if MOMENT_SCALE is None:
            s = _jnp.sqrt(_jnp.mean(_jnp.square(w)) + 1e-30)
        else:
            s = MOMENT_SCALE[name]
        km, kv = _jax.random.split(_jax.random.fold_in(key, i + 1))
        out[name] = w
        out["m_" + name] = s * _jax.random.normal(km, w.shape, _jnp.float32)
        out["v_" + name] = (s * s) * _jax.random.uniform(kv, w.shape, _jnp.float32, 0.5, 1.5)
    if N_MICROBATCH > 1:
        for name, axis in PER_EXAMPLE_BATCH_AXIS.items():
            out[name] = _to_microbatches(out[name], axis)
    return {'x': out['x'], 'norm_mix': out['norm_mix'], 'norm_ffn': out['norm_ffn'], 'norm_final': out['norm_final'], 'm_w_in': out['m_w_in'], 'm_b_if': out['m_b_if'], 'm_head_norm': out['m_head_norm'], 'm_w_out': out['m_w_out'], 'r_w_in': out['r_w_in'], 'r_conv_w': out['r_conv_w'], 'r_conv_b': out['r_conv_b'], 'r_gate_w': out['r_gate_w'], 'r_gate_b': out['r_gate_b'], 'r_a_param': out['r_a_param'], 'r_w_out': out['r_w_out'], 'ffn_w_in': out['ffn_w_in'], 'ffn_w_out': out['ffn_w_out'], 'loss_target': out['loss_target'], 'm_norm_mix': out['m_norm_mix'], 'm_norm_ffn': out['m_norm_ffn'], 'm_norm_final': out['m_norm_final'], 'm_m_w_in': out['m_m_w_in'], 'm_m_b_if': out['m_m_b_if'], 'm_m_head_norm': out['m_m_head_norm'], 'm_m_w_out': out['m_m_w_out'], 'm_r_w_in': out['m_r_w_in'], 'm_r_conv_w': out['m_r_conv_w'], 'm_r_conv_b': out['m_r_conv_b'], 'm_r_gate_w': out['m_r_gate_w'], 'm_r_gate_b': out['m_r_gate_b'], 'm_r_a_param': out['m_r_a_param'], 'm_r_w_out': out['m_r_w_out'], 'm_ffn_w_in': out['m_ffn_w_in'], 'm_ffn_w_out': out['m_ffn_w_out'], 'v_norm_mix': out['v_norm_mix'], 'v_norm_ffn': out['v_norm_ffn'], 'v_norm_final': out['v_norm_final'], 'v_m_w_in': out['v_m_w_in'], 'v_m_b_if': out['v_m_b_if'], 'v_m_head_norm': out['v_m_head_norm'], 'v_m_w_out': out['v_m_w_out'], 'v_r_w_in': out['v_r_w_in'], 'v_r_conv_w': out['v_r_conv_w'], 'v_r_conv_b': out['v_r_conv_b'], 'v_r_gate_w': out['v_r_gate_w'], 'v_r_gate_b': out['v_r_gate_b'], 'v_r_a_param': out['v_r_a_param'], 'v_r_w_out': out['v_r_w_out'], 'v_ffn_w_in': out['v_ffn_w_in'], 'v_ffn_w_out': out['v_ffn_w_out']}


def _loss(weights, diff, rest, loss_target):
    with _jax.named_scope("forward"):
        args = {**rest, TWIN_DIFF_INPUT: diff, **{k: w.astype(_WEIGHT_DTYPES[k]) for k, w in weights.items()}}
        y = _forward(args)
    with _jax.named_scope("loss_head"):
        err = _jnp.square(y.astype(_jnp.float32) - loss_target)
        return 0.5 * _jnp.sum(_jnp.mean(err, axis=-1)) if err.ndim else 0.5 * err


def _adamw(w, g, m, v):
    m = ADAM_B1 * m + (1.0 - ADAM_B1) * g
    v = ADAM_B2 * v + (1.0 - ADAM_B2) * _jnp.square(g)
    m_hat = m / (1.0 - ADAM_B1 ** ADAM_STEP)
    v_hat = v / (1.0 - ADAM_B2 ** ADAM_STEP)
    delta = -ADAM_LR * (m_hat / (_jnp.sqrt(v_hat) + ADAM_EPS) + ADAM_WD * w)
    return delta, m, v


def reference(x, norm_mix, norm_ffn, norm_final, m_w_in, m_b_if, m_head_norm, m_w_out, r_w_in, r_conv_w, r_conv_b, r_gate_w, r_gate_b, r_a_param, r_w_out, ffn_w_in, ffn_w_out, loss_target, m_norm_mix, m_norm_ffn, m_norm_final, m_m_w_in, m_m_b_if, m_m_head_norm, m_m_w_out, m_r_w_in, m_r_conv_w, m_r_conv_b, m_r_gate_w, m_r_gate_b, m_r_a_param, m_r_w_out, m_ffn_w_in, m_ffn_w_out, v_norm_mix, v_norm_ffn, v_norm_final, v_m_w_in, v_m_b_if, v_m_head_norm, v_m_w_out, v_r_w_in, v_r_conv_w, v_r_conv_b, v_r_gate_w, v_r_gate_b, v_r_a_param, v_r_w_out, v_ffn_w_in, v_ffn_w_out):
    given = dict(x=x, norm_mix=norm_mix, norm_ffn=norm_ffn, norm_final=norm_final, m_w_in=m_w_in, m_b_if=m_b_if, m_head_norm=m_head_norm, m_w_out=m_w_out, r_w_in=r_w_in, r_conv_w=r_conv_w, r_conv_b=r_conv_b, r_gate_w=r_gate_w, r_gate_b=r_gate_b, r_a_param=r_a_param, r_w_out=r_w_out, ffn_w_in=ffn_w_in, ffn_w_out=ffn_w_out, loss_target=loss_target, m_norm_mix=m_norm_mix, m_norm_ffn=m_norm_ffn, m_norm_final=m_norm_final, m_m_w_in=m_m_w_in, m_m_b_if=m_m_b_if, m_m_head_norm=m_m_head_norm, m_m_w_out=m_m_w_out, m_r_w_in=m_r_w_in, m_r_conv_w=m_r_conv_w, m_r_conv_b=m_r_conv_b, m_r_gate_w=m_r_gate_w, m_r_gate_b=m_r_gate_b, m_r_a_param=m_r_a_param, m_r_w_out=m_r_w_out, m_ffn_w_in=m_ffn_w_in, m_ffn_w_out=m_ffn_w_out, v_norm_mix=v_norm_mix, v_norm_ffn=v_norm_ffn, v_norm_final=v_norm_final, v_m_w_in=v_m_w_in, v_m_b_if=v_m_b_if, v_m_head_norm=v_m_head_norm, v_m_w_out=v_m_w_out, v_r_w_in=v_r_w_in, v_r_conv_w=v_r_conv_w, v_r_conv_b=v_r_conv_b, v_r_gate_w=v_r_gate_w, v_r_gate_b=v_r_gate_b, v_r_a_param=v_r_a_param, v_r_w_out=v_r_w_out, v_ffn_w_in=v_ffn_w_in, v_ffn_w_out=v_ffn_w_out)
    weights = {n: given[n] for n in TWIN_WEIGHTS}
    shared = {n: given[n] for n in SHARED_INPUTS}
    per_example = {n: given[n] for n in ['x']}
    grad_fn = _jax.value_and_grad(_loss, argnums=(0, 1))

    def one_microbatch(ex, loss_target):
        ex = dict(ex)
        diff = ex.pop(TWIN_DIFF_INPUT)
        return grad_fn(weights, diff, {**shared, **ex}, loss_target)

    if N_MICROBATCH == 1:
        loss, (grad_w, grad_x) = one_microbatch(per_example, given["loss_target"])
    else:
        def body(carry, xs):
            loss_sum, grad_sum = carry
            l_k, (gw_k, gx_k) = one_microbatch(xs[0], xs[1])
            with _jax.named_scope("update"):
                return (loss_sum + l_k, _jax.tree.map(_jnp.add, grad_sum, gw_k)), gx_k

        init = (_jnp.zeros((), _jnp.float32), _jax.tree.map(_jnp.zeros_like, weights))
        (loss, grad_w), grad_x = _jax.lax.scan(body, init, (per_example, given["loss_target"]))
    with _jax.named_scope("update"):
        delta_w, new_m, new_v = {}, {}, {}
        for n in TWIN_WEIGHTS:
            delta_w[n], new_m[n], new_v[n] = _adamw(weights[n], grad_w[n], given["m_" + n], given["v_" + n])
    return (loss, grad_x, *[grad_w[n] for n in TWIN_WEIGHTS], *[delta_w[n] for n in TWIN_WEIGHTS],
            *[new_m[n] for n in TWIN_WEIGHTS], *[new_v[n] for n in TWIN_WEIGHTS])
```

```python
import functools
import math

import jax
import jax.numpy as jnp
from jax import lax
from jax.experimental import pallas as pl
from jax.experimental.pallas import tpu as pltpu

F32 = jnp.float32
BF16 = jnp.bfloat16

EPS = 1e-6
HEADS = 8
CHUNK = 64
RBLOCKS = 8
CONV_W = 4
R_C = 8.0
LANES = 128
IF_PAD = LANES
VMEM_LIMIT = 52 << 20

ADAM_LR = 0.001
ADAM_B1 = 0.9
ADAM_B2 = 0.999
ADAM_EPS = 1e-08
ADAM_WD = 0.01
ADAM_STEP = 10

_DIMS = {
    "nn": (((1,), (0,)), ((), ())),
    "nt": (((1,), (1,)), ((), ())),
    "tn": (((0,), (0,)), ((), ())),
}


def _dot(a, b, mode="nn"):
    return lax.dot_general(a, b, _DIMS[mode], preferred_element_type=F32)


def _pick(n, pref, mult):
    best = None
    d = mult
    while d <= min(n, pref):
        if n % d == 0:
            best = d
        d += mult
    return best if best is not None else n


def _cparams(sem):
    return pltpu.CompilerParams(dimension_semantics=sem, vmem_limit_bytes=VMEM_LIMIT)


def _mm(name, a, b, mode, out_defs, *, tm, tn, tk, epi=None, extras=()):
    if mode == "nn":
        M, K = a.shape
        N = b.shape[1]
        a_spec = pl.BlockSpec((tm, tk), lambda i, j, k: (i, k))
        b_spec = pl.BlockSpec((tk, tn), lambda i, j, k: (k, j))
    elif mode == "nt":
        M, K = a.shape
        N = b.shape[0]
        a_spec = pl.BlockSpec((tm, tk), lambda i, j, k: (i, k))
        b_spec = pl.BlockSpec((tn, tk), lambda i, j, k: (j, k))
    else:
        K, M = a.shape
        N = b.shape[1]
        a_spec = pl.BlockSpec((tk, tm), lambda i, j, k: (k, i))
        b_spec = pl.BlockSpec((tk, tn), lambda i, j, k: (k, j))
    assert M % tm == 0 and N % tn == 0 and K % tk == 0, (name, M, N, K, tm, tn, tk)
    nk = K // tk
    ne, no = len(extras), len(out_defs)

    def tile_spec(cols):
        return pl.BlockSpec((tm, cols * tn // N), lambda i, j, k: (i, j))

    def body(*refs):
        a_ref, b_ref = refs[0], refs[1]
        ex = refs[2:2 + ne]
        outs = refs[2 + ne:2 + ne + no]

        def finish(acc):
            res = epi(acc, *[e[...] for e in ex]) if epi is not None else (acc,)
            for o_ref, r in zip(outs, res):
                o_ref[...] = r.astype(o_ref.dtype)

        d = _dot(a_ref[...].astype(BF16), b_ref[...].astype(BF16), mode)
        if nk == 1:
            finish(d)
        else:
            acc_ref = refs[-1]
            k = pl.program_id(2)

            @pl.when(k == 0)
            def _():
                acc_ref[...] = d

            @pl.when(k > 0)
            def _():
                acc_ref[...] += d

            @pl.when(k == nk - 1)
            def _():
                finish(acc_ref[...])

    res = pl.pallas_call(
        body, name=name, grid=(M // tm, N // tn, nk),
        in_specs=[a_spec, b_spec] + [tile_spec(e.shape[1]) for e in extras],
        out_specs=[tile_spec(c) for _, c in out_defs],
        out_shape=[jax.ShapeDtypeStruct((M, c), dt) for dt, c in out_defs],
        scratch_shapes=[pltpu.VMEM((tm, tn), F32)] if nk > 1 else [],
        compiler_params=_cparams(("parallel", "parallel", "arbitrary")),
    )(a, b, *extras)
    return res


def _epi_resid(acc, resid):
    return (resid + acc,)


def _sigmoid(x):
    return 1.0 / (1.0 + jnp.exp(-x))


def _make_epi_swiglu(gran):
    def epi(acc):
        parts = []
        for p in range(acc.shape[1] // (2 * gran)):
            g = acc[:, (2 * p) * gran:(2 * p + 1) * gran]
            u = acc[:, (2 * p + 1) * gran:(2 * p + 2) * gran]
            parts.append(g * _sigmoid(g) * u)
        a = parts[0] if len(parts) == 1 else jnp.concatenate(parts, axis=1)
        return acc, a
    return epi


def _make_epi_swiglu_bwd(gran):
    def epi(da, gu):
        parts = []
        for p in range(da.shape[1] // gran):
            d = da[:, p * gran:(p + 1) * gran]
            g = gu[:, (2 * p) * gran:(2 * p + 1) * gran].astype(F32)
            u = gu[:, (2 * p + 1) * gran:(2 * p + 2) * gran].astype(F32)
            sig = _sigmoid(g)
            parts.append(d * u * (sig * (1.0 + g * (1.0 - sig))))
            parts.append(d * (g * sig))
        return (jnp.concatenate(parts, axis=1),)
    return epi


def _rms_fwd(name, x, g):
    T, D = x.shape
    tr = _pick(T, 512, 16)

    def body(x_ref, g_ref, hn_ref, r_ref):
        xv = x_ref[...]
        r = lax.rsqrt(jnp.mean(xv * xv, axis=-1, keepdims=True) + EPS)
        hn_ref[...] = (xv * r * g_ref[...]).astype(BF16)
        r_ref[...] = r

    return pl.pallas_call(
        body, name=name, grid=(T // tr,),
        in_specs=[pl.BlockSpec((tr, D), lambda i: (i, 0)), pl.BlockSpec((1, D), lambda i: (0, 0))],
        out_specs=[pl.BlockSpec((tr, D), lambda i: (i, 0)), pl.BlockSpec((tr, 1), lambda i: (i, 0))],
        out_shape=[jax.ShapeDtypeStruct((T, D), BF16), jax.ShapeDtypeStruct((T, 1), F32)],
        compiler_params=_cparams(("parallel",)),
    )(x, g)


def _rms_bwd(name, x, r, dy, g, dres):
    T, D = x.shape
    tr = _pick(T, 512, 16)

    def body(x_ref, r_ref, dy_ref, g_ref, dres_ref, dh_ref, dhb_ref, dg_ref):
        xh = x_ref[...] * r_ref[...]
        dy_v = dy_ref[...]
        dyg = dy_v * g_ref[...]
        m = jnp.mean(dyg * xh, axis=-1, keepdims=True)
        dh = dres_ref[...] + r_ref[...] * (dyg - xh * m)
        dh_ref[...] = dh
        dhb_ref[...] = dh.astype(BF16)

        @pl.when(pl.program_id(0) == 0)
        def _():
            dg_ref[...] = jnp.zeros_like(dg_ref)

        dg_ref[...] += jnp.sum(dy_v * xh, axis=0, keepdims=True)

    row = pl.BlockSpec((tr, D), lambda i: (i, 0))
    vec = pl.BlockSpec((1, D), lambda i: (0, 0))
    return pl.pallas_call(
        body, name=name, grid=(T // tr,),
        in_specs=[row, pl.BlockSpec((tr, 1), lambda i: (i, 0)), row, vec, row],
        out_specs=[row, row, vec],
        out_shape=[jax.ShapeDtypeStruct((T, D), F32), jax.ShapeDtypeStruct((T, D), BF16),
                   jax.ShapeDtypeStruct((1, D), F32)],
        compiler_params=_cparams(("arbitrary",)),
    )(x, r, dy, g, dres)


def _loss_head(x, g, tgt):
    T, D = x.shape
    tr = _pick(T, 512, 16)

    def body(x_ref, g_ref, t_ref, loss_ref, dh_ref, dhb_ref, dg_ref):
        xv = x_ref[...]
        gv = g_ref[...]
        r = lax.rsqrt(jnp.mean(xv * xv, axis=-1, keepdims=True) + EPS)
        xh = xv * r
        e = xh * gv - t_ref[...]
        part = 0.5 * jnp.sum(jnp.mean(e * e, axis=-1, keepdims=True), axis=0, keepdims=True)
        dy = e * (1.0 / D)
        dyg = dy * gv
        m = jnp.mean(dyg * xh, axis=-1, keepdims=True)
        dh = r * (dyg - xh * m)
        dh_ref[...] = dh
        dhb_ref[...] = dh.astype(BF16)

        @pl.when(pl.program_id(0) == 0)
        def _():
            dg_ref[...] = jnp.zeros_like(dg_ref)
            loss_ref[...] = jnp.zeros_like(loss_ref)

        dg_ref[...] += jnp.sum(dy * xh, axis=0, keepdims=True)
        loss_ref[...] += jnp.broadcast_to(part, loss_ref.shape)

    row = pl.BlockSpec((tr, D), lambda i: (i, 0))
    vec = pl.BlockSpec((1, D), lambda i: (0, 0))
    return pl.pallas_call(
        body, name="loss_head", grid=(T // tr,),
        in_specs=[row, vec, row],
        out_specs=[pl.BlockSpec((1, LANES), lambda i: (0, 0)), row, row, vec],
        out_shape=[jax.ShapeDtypeStruct((1, LANES), F32), jax.ShapeDtypeStruct((T, D), F32),
                   jax.ShapeDtypeStruct((T, D), BF16), jax.ShapeDtypeStruct((1, D), F32)],
        compiler_params=_cparams(("arbitrary",)),
    )(x, g, tgt)


def _mlstm_gates(proj, b_if_pad, D):
    T = proj.shape[0]
    tr = _pick(T, 512, CHUNK)

    def body(p_ref, b_ref, o_ref):
        z = p_ref[...] + b_ref[...]
        lane = lax.broadcasted_iota(jnp.int32, z.shape, 1)
        row = lax.broadcasted_iota(jnp.int32, z.shape, 0) % CHUNK
        lf = jnp.where((lane >= HEADS) & (lane < 2 * HEADS),
                       jnp.minimum(z, 0.0) - jnp.log(1.0 + jnp.exp(-jnp.abs(z))), 0.0)
        c = lf
        d = 1
        while d < CHUNK:
            c = c + jnp.where(row >= d, pltpu.roll(c, d, 0), 0.0)
            d *= 2
        o_ref[...] = jnp.where(lane < HEADS, z, 0.0) + lf + pltpu.roll(c, HEADS, 1)

    return pl.pallas_call(
        body, name="mlstm_gates", grid=(T // tr,),
        in_specs=[pl.BlockSpec((tr, IF_PAD), lambda i: (i, 3 * D // IF_PAD)),
                  pl.BlockSpec((1, IF_PAD), lambda i: (0, 0))],
        out_specs=pl.BlockSpec((tr, IF_PAD), lambda i: (i, 0)),
        out_shape=jax.ShapeDtypeStruct((T, IF_PAD), F32),
        compiler_params=_cparams(("parallel",)),
    )(proj, b_if_pad)


def _mlstm_chunk_common(qf, kf, vf, gc, gr, h, c_prev, n_prev, m_prev, causal):
    L = CHUNK
    qb = qf.astype(BF16)
    kb = kf.astype(BF16)
    vb = vf.astype(BF16)
    i_col = gc[:, h:h + 1]
    b_col = gc[:, 2 * HEADS + h:2 * HEADS + h + 1]
    i_row = gr[h:h + 1, :]
    b_row = gr[2 * HEADS + h:2 * HEADS + h + 1, :]
    dmat = jnp.where(causal, b_col - b_row + i_row, -jnp.inf)
    m_inter = b_col + m_prev
    m_t = jnp.maximum(m_inter, jnp.max(dmat, axis=-1, keepdims=True))
    w = jnp.exp(dmat - m_t)
    qk = _dot(qb, kb, "nt")
    s = qk * w
    sc = jnp.exp(m_inter - m_t)
    cb = c_prev.astype(BF16)
    num_inter = _dot(qb, cb)
    nb = n_prev.astype(BF16).astype(F32)
    qn = jnp.sum(qb.astype(F32) * nb, axis=-1, keepdims=True)
    num = _dot(s.astype(BF16), vb) + sc * num_inter
    den = jnp.sum(s, axis=-1, keepdims=True) + sc * qn
    e_m = jnp.exp(-m_t)
    nrm = jnp.maximum(jnp.abs(den), e_m)
    hh = num / nrm
    b_last = b_row[:, L - 1:L]
    g_row = b_last - b_row + i_row
    g_col = b_last - b_col + i_col
    m_new = jnp.maximum(b_last + m_prev, jnp.max(g_row, axis=-1, keepdims=True))
    wk = jnp.exp(g_col - m_new)
    decay = jnp.exp(b_last + m_prev - m_new)
    kw = kf * wk
    return dict(qb=qb, kb=kb, vb=vb, w=w, s=s, sc=sc, cb=cb, nb=nb, qn=qn, num_inter=num_inter,
                den=den, e_m=e_m, nrm=nrm, hh=hh, m_new=m_new, wk=wk, decay=decay, kw=kw)


def _mlstm_specs(T, D):
    qk = D // 2
    row = lambda w, j: pl.BlockSpec((CHUNK, w), lambda c, w=w, j=j: (c, j))
    return [row(qk, 0), row(qk, 1), row(D, 1), row(D, 2)]


def _mlstm_fwd(proj, gcol, grow, head_norm):
    T = proj.shape[0]
    D = head_norm.shape[1]
    nc = T // CHUNK
    dk, dv = D // 2 // HEADS, D // HEADS
    kscale = dk ** -0.5
    L = CHUNK

    def body(q_ref, k_ref, v_ref, o_ref, gc_ref, gr_ref, hn_ref, hg_ref, cs_ref, ns_ref, ms_ref, c_s, n_s, m_s):
        @pl.when(pl.program_id(0) == 0)
        def _():
            c_s[...] = jnp.zeros_like(c_s)
            n_s[...] = jnp.zeros_like(n_s)
            m_s[...] = jnp.zeros_like(m_s)

        cs_ref[0] = c_s[...]
        ns_ref[0] = n_s[...]
        ms_ref[0] = m_s[...]
        causal = lax.broadcasted_iota(jnp.int32, (L, L), 1) <= lax.broadcasted_iota(jnp.int32, (L, L), 0)
        gc = gc_ref[...]
        gr = gr_ref[0]
        for h in range(HEADS):
            qf = q_ref[:, h * dk:(h + 1) * dk]
            kf = k_ref[:, h * dk:(h + 1) * dk] * kscale
            vf = v_ref[:, h * dv:(h + 1) * dv]
            m_prev = m_s[h:h + 1, 0:1]
            f = _mlstm_chunk_common(qf, kf, vf, gc, gr, h, c_s[h], n_s[h:h + 1, :], m_prev, causal)
            c_s[h] = f["decay"] * c_s[h] + _dot(f["kw"].astype(BF16), f["vb"], "tn")
            n_s[h:h + 1, :] = f["decay"] * n_s[h:h + 1, :] + jnp.sum(f["kw"], axis=0, keepdims=True)
            m_s[h:h + 1, :] = jnp.broadcast_to(f["m_new"], (1, LANES))
            hh = f["hh"]
            rs = lax.rsqrt(jnp.mean(hh * hh, axis=-1, keepdims=True) + EPS)
            sl = slice(h * dv, (h + 1) * dv)
            hg_ref[:, sl] = (hh * rs * hn_ref[:, sl] * _sigmoid(o_ref[:, sl])).astype(BF16)

    return pl.pallas_call(
        body, name="mlstm_fwd", grid=(nc,),
        in_specs=_mlstm_specs(T, D) + [
            pl.BlockSpec((L, IF_PAD), lambda c: (c, 0)),
            pl.BlockSpec((1, 3 * HEADS, L), lambda c: (c, 0, 0)),
            pl.BlockSpec((1, D), lambda c: (0, 0)),
        ],
        out_specs=[
            pl.BlockSpec((L, D), lambda c: (c, 0)),
            pl.BlockSpec((1, HEADS, dk, dv), lambda c: (c, 0, 0, 0)),
            pl.BlockSpec((1, HEADS, dk), lambda c: (c, 0, 0)),
            pl.BlockSpec((1, HEADS, LANES), lambda c: (c, 0, 0)),
        ],
        out_shape=[
            jax.ShapeDtypeStruct((T, D), BF16),
            jax.ShapeDtypeStruct((nc, HEADS, dk, dv), F32),
            jax.ShapeDtypeStruct((nc, HEADS, dk), F32),
            jax.ShapeDtypeStruct((nc, HEADS, LANES), F32),
        ],
        scratch_shapes=[pltpu.VMEM((HEADS, dk, dv), F32), pltpu.VMEM((HEADS, dk), F32),
                        pltpu.VMEM((HEADS, LANES), F32)],
        compiler_params=_cparams(("arbitrary",)),
    )(proj, proj, proj, proj, gcol, grow, head_norm)


def _mlstm_bwd(proj, gcol, grow, head_norm, dhg, cs, ns, ms):
    T = proj.shape[0]
    D = head_norm.shape[1]
    nc = T // CHUNK
    dk, dv = D // 2 // HEADS, D // HEADS
    qkw = D // 2
    kscale = dk ** -0.5
    L = CHUNK
    pw = 3 * D + IF_PAD

    def body(q_ref, k_ref, v_ref, o_ref, gc_ref, gr_ref, hn_ref, dhg_ref, cs_ref, ns_ref, ms_ref,
             dp_ref, dgain_ref, dbif_ref, dc_s, dn_s):
        @pl.when(pl.program_id(0) == 0)
        def _():
            dc_s[...] = jnp.zeros_like(dc_s)
            dn_s[...] = jnp.zeros_like(dn_s)
            dgain_ref[...] = jnp.zeros_like(dgain_ref)
            dbif_ref[...] = jnp.zeros_like(dbif_ref)

        rowl = lax.broadcasted_iota(jnp.int32, (L, L), 0)
        coll = lax.broadcasted_iota(jnp.int32, (L, L), 1)
        causal = coll <= rowl
        eye = coll == rowl
        lane = lax.broadcasted_iota(jnp.int32, (L, IF_PAD), 1)
        rowg = lax.broadcasted_iota(jnp.int32, (L, IF_PAD), 0)
        gc = gc_ref[...]
        gr = gr_ref[0]
        dgate = jnp.zeros((L, IF_PAD), F32)
        for h in range(HEADS):
            qf = q_ref[:, h * dk:(h + 1) * dk]
            kf = k_ref[:, h * dk:(h + 1) * dk] * kscale
            vf = v_ref[:, h * dv:(h + 1) * dv]
            c_prev = cs_ref[0, h]
            n_prev = ns_ref[0, h:h + 1, :]
            m_prev = ms_ref[0, h:h + 1, 0:1]
            f = _mlstm_chunk_common(qf, kf, vf, gc, gr, h, c_prev, n_prev, m_prev, causal)
            qb, kb, vb, w, s, sc = f["qb"], f["kb"], f["vb"], f["w"], f["s"], f["sc"]
            hh, nrm, den = f["hh"], f["nrm"], f["den"]
            sl = slice(h * dv, (h + 1) * dv)
            dhg_v = dhg_ref[:, sl]
            sig = _sigmoid(o_ref[:, sl])
            gain = hn_ref[:, sl]
            rs = lax.rsqrt(jnp.mean(hh * hh, axis=-1, keepdims=True) + EPS)
            hn_v = hh * rs
            d_o = dhg_v * hn_v * gain * sig * (1.0 - sig)
            dgain_ref[:, sl] += jnp.sum(dhg_v * hn_v * sig, axis=0, keepdims=True)
            dhn = dhg_v * gain * sig
            dh = rs * (dhn - hn_v * jnp.mean(dhn * hn_v, axis=-1, keepdims=True))
            dnum = dh / nrm
            ddn = -jnp.sum(dh * hh, axis=-1, keepdims=True) / nrm
            dden = jnp.where(jnp.abs(den) > f["e_m"], jnp.where(den > 0.0, ddn, -ddn), 0.0)
            dnum_b = dnum.astype(BF16)
            d_s = _dot(dnum_b, vb, "nt") + dden
            dqk = (d_s * w).astype(BF16)
            p = d_s * s
            dcn = dc_s[h]
            dcb = dcn.astype(BF16)
            dnn = dn_s[h:h + 1, :]
            kwb = f["kw"].astype(BF16)
            d_v = _dot(s.astype(BF16), dnum_b, "tn") + _dot(kwb, dcb)
            dkw = _dot(vb, dcb, "nt") + dnn
            d_q = _dot(dqk, kb) + sc * (_dot(dnum_b, f["cb"], "nt") + dden * f["nb"])
            d_kf = _dot(dqk, qb, "tn") + f["wk"] * dkw
            u_col = f["wk"] * jnp.sum(kf * dkw, axis=-1, keepdims=True)
            z = f["decay"] * (jnp.sum(jnp.sum(c_prev * dcn, axis=-1, keepdims=True), axis=0, keepdims=True)
                              + jnp.sum(n_prev * dnn, axis=-1, keepdims=True))
            r_col = sc * (jnp.sum(dnum * f["num_inter"], axis=-1, keepdims=True) + dden * f["qn"])
            rowsum_p = jnp.sum(p, axis=-1, keepdims=True)
            colsum_p = jnp.sum(p, axis=0, keepdims=True)
            colsum_col = jnp.sum(jnp.where(eye, colsum_p, 0.0), axis=-1, keepdims=True)
            di_col = colsum_col + u_col
            db_col = rowsum_p + r_col - colsum_col - u_col
            db_last = jnp.sum(u_col, axis=0, keepdims=True) + z
            qs = (qb.astype(F32) * sc)
            dc_s[h] = f["decay"] * dcn + _dot(qs.astype(BF16), dnum_b, "tn")
            dn_s[h:h + 1, :] = f["decay"] * dnn + jnp.sum(qs * dden, axis=0, keepdims=True)
            dgate = jnp.where(lane == h, di_col, dgate)
            dgate = jnp.where(lane == 2 * HEADS + h,
                              db_col + jnp.where(rowg == L - 1, db_last, 0.0), dgate)
            dp_ref[:, h * dk:(h + 1) * dk] = d_q.astype(BF16)
            dp_ref[:, qkw + h * dk:qkw + (h + 1) * dk] = (d_kf * kscale).astype(BF16)
            dp_ref[:, D + h * dv:D + (h + 1) * dv] = d_v.astype(BF16)
            dp_ref[:, 2 * D + h * dv:2 * D + (h + 1) * dv] = d_o.astype(BF16)
        rc = jnp.where(lane >= 2 * HEADS, dgate, 0.0)
        d = 1
        while d < L:
            rc = rc + jnp.where(rowg < L - d, pltpu.roll(rc, L - d, 0), 0.0)
            d *= 2
        dlf = pltpu.roll(rc, IF_PAD - HEADS, 1)
        dif = jnp.where(lane < HEADS, dgate, 0.0) + jnp.where(
            (lane >= HEADS) & (lane < 2 * HEADS), dlf * (1.0 - jnp.exp(gc)), 0.0)
        dp_ref[:, 3 * D:3 * D + IF_PAD] = dif.astype(BF16)
        dbif_ref[...] += jnp.sum(dif, axis=0, keepdims=True)

    rev = lambda c: nc - 1 - c
    specs = [pl.BlockSpec(s.block_shape, (lambda c, f=s.index_map: f(rev(c)))) for s in _mlstm_specs(T, D)]
    return pl.pallas_call(
        body, name="mlstm_bwd", grid=(nc,),
        in_specs=specs + [
            pl.BlockSpec((L, IF_PAD), lambda c: (rev(c), 0)),
            pl.BlockSpec((1, 3 * HEADS, L), lambda c: (rev(c), 0, 0)),
            pl.BlockSpec((1, D), lambda c: (0, 0)),
            pl.BlockSpec((L, D), lambda c: (rev(c), 0)),
            pl.BlockSpec((1, HEADS, dk, dv), lambda c: (rev(c), 0, 0, 0)),
            pl.BlockSpec((1, HEADS, dk), lambda c: (rev(c), 0, 0)),
            pl.BlockSpec((1, HEADS, LANES), lambda c: (rev(c), 0, 0)),
        ],
        out_specs=[
            pl.BlockSpec((L, pw), lambda c: (rev(c), 0)),
            pl.BlockSpec((1, D), lambda c: (0, 0)),
            pl.BlockSpec((1, IF_PAD), lambda c: (0, 0)),
        ],
        out_shape=[
            jax.ShapeDtypeStruct((T, pw), BF16),
            jax.ShapeDtypeStruct((1, D), F32),
            jax.ShapeDtypeStruct((1, IF_PAD), F32),
        ],
        scratch_shapes=[pltpu.VMEM((HEADS, dk, dv), F32), pltpu.VMEM((HEADS, dk), F32)],
        compiler_params=_cparams(("arbitrary",)),
    )(proj, proj, proj, proj, gcol, grow, head_norm, dhg, cs, ns, ms)


_GELU_C = math.sqrt(2.0 / math.pi)


def _log_sigmoid(x):
    return jnp.minimum(x, 0.0) - jnp.log(1.0 + jnp.exp(-jnp.abs(x)))


def _rglru_recompute(pr_ref, halo_ref, cw_ref, cb_ref, gw_ref, gb_ref, ap_ref, first, W, Lt):
    x = pr_ref[:, W:2 * W]
    halo = jnp.where(first, 0.0, halo_ref[:, W:2 * W])
    xe = jnp.concatenate([halo, x], axis=0)
    xs = [pltpu.roll(xe, d, 0)[8:8 + Lt] for d in (3, 2, 1)] + [x]
    cw = cw_ref[...]
    rec_c = cb_ref[...] + cw[0:1] * xs[0] + cw[1:2] * xs[1] + cw[2:3] * xs[2] + cw[3:4] * xs[3]
    gates = _dot(rec_c.astype(BF16), gw_ref[0]) + gb_ref[0]
    r = _sigmoid(gates[:, :W])
    ig = _sigmoid(gates[:, W:])
    lsa = _log_sigmoid(ap_ref[...])
    log_a = R_C * r * lsa
    a = jnp.exp(log_a)
    mult = jnp.sqrt(1.0 - jnp.exp(2.0 * log_a))
    return dict(xs=xs, cw=cw, rec_c=rec_c, r=r, ig=ig, lsa=lsa, a=a, mult=mult)


def _rglru_specs(T, D, Lt, tmap):
    W = D // RBLOCKS
    hb = Lt // 8
    return [
        pl.BlockSpec((Lt, 2 * W), lambda g, i: (tmap(i), g)),
        pl.BlockSpec((8, 2 * W), lambda g, i: (jnp.maximum(tmap(i) * hb - 1, 0), g)),
        pl.BlockSpec((CONV_W, W), lambda g, i: (0, g)),
        pl.BlockSpec((1, W), lambda g, i: (0, g)),
        pl.BlockSpec((1, W, 2 * W), lambda g, i: (g, 0, 0)),
        pl.BlockSpec((1, 1, 2 * W), lambda g, i: (g, 0, 0)),
        pl.BlockSpec((1, W), lambda g, i: (0, g)),
    ]


def _rglru_fwd(pr, conv_w, conv_b, gate_w, gate_b, a_param):
    T = pr.shape[0]
    D = pr.shape[1] // 2
    W = D // RBLOCKS
    Lt = _pick(T, 256, 8)
    nt = T // Lt

    def body(pr_ref, halo_ref, cw_ref, cb_ref, gw_ref, gb_ref, ap_ref, y_ref, hs_ref, carry):
        i = pl.program_id(1)

        @pl.when(i == 0)
        def _():
            carry[...] = jnp.zeros_like(carry)

        f = _rglru_recompute(pr_ref, halo_ref, cw_ref, cb_ref, gw_ref, gb_ref, ap_ref, i == 0, W, Lt)
        row = lax.broadcasted_iota(jnp.int32, (Lt, W), 0)
        a_c = f["a"]
        u_c = f["mult"] * (f["ig"] * f["rec_c"])
        d = 1
        while d < Lt:
            msk = row >= d
            u_c = jnp.where(msk, a_c * pltpu.roll(u_c, d, 0) + u_c, u_c)
            a_c = jnp.where(msk, a_c * pltpu.roll(a_c, d, 0), a_c)
            d *= 2
        h = u_c + a_c * carry[0:1, :]
        carry[0:1, :] = h[Lt - 1:Lt, :]
        hs_ref[...] = h
        gb = pr_ref[:, :W]
        t = jnp.tanh(_GELU_C * (gb + 0.044715 * gb * gb * gb))
        y_ref[...] = (0.5 * gb * (1.0 + t) * h).astype(BF16)

    blk = pl.BlockSpec((Lt, W), lambda g, i: (i, g))
    return pl.pallas_call(
        body, name="rglru_fwd", grid=(RBLOCKS, nt),
        in_specs=_rglru_specs(T, D, Lt, lambda i: i),
        out_specs=[blk, blk],
        out_shape=[jax.ShapeDtypeStruct((T, D), BF16), jax.ShapeDtypeStruct((T, D), F32)],
        scratch_shapes=[pltpu.VMEM((8, W), F32)],
        compiler_params=_cparams(("parallel", "arbitrary")),
    )(pr, pr, conv_w, conv_b, gate_w, gate_b, a_param)


def _rglru_bwd(pr, hs, dy, conv_w, conv_b, gate_w, gate_b, a_param):
    T = pr.shape[0]
    D = pr.shape[1] // 2
    W = D // RBLOCKS
    Lt = _pick(T, 256, 8)
    nt = T // Lt
    hb = Lt // 8
    tmap = lambda i: nt - 1 - i

    def body(pr_ref, halo_ref, cw_ref, cb_ref, gw_ref, gb_ref, ap_ref, hs_ref, hh_ref, dy_ref,
             dpr_ref, dcw_ref, dcb_ref, dgw_ref, dgb_ref, dap_ref, lam_s, drc_s):
        i = pl.program_id(1)
        ti = nt - 1 - i

        @pl.when(i == 0)
        def _():
            lam_s[...] = jnp.zeros_like(lam_s)
            drc_s[...] = jnp.zeros_like(drc_s)
            dcw_ref[...] = jnp.zeros_like(dcw_ref)
            dcb_ref[...] = jnp.zeros_like(dcb_ref)
            dgw_ref[...] = jnp.zeros_like(dgw_ref)
            dgb_ref[...] = jnp.zeros_like(dgb_ref)
            dap_ref[...] = jnp.zeros_like(dap_ref)

        f = _rglru_recompute(pr_ref, halo_ref, cw_ref, cb_ref, gw_ref, gb_ref, ap_ref, ti == 0, W, Lt)
        a, mult, ig, r, rec_c, lsa, xs, cw = (f[k] for k in ("a", "mult", "ig", "r", "rec_c", "lsa", "xs", "cw"))
        row = lax.broadcasted_iota(jnp.int32, (Lt, W), 0)
        gb = pr_ref[:, :W]
        t = jnp.tanh(_GELU_C * (gb + 0.044715 * gb * gb * gb))
        gel = 0.5 * gb * (1.0 + t)
        dgel = 0.5 * (1.0 + t) + 0.5 * gb * (1.0 - t * t) * _GELU_C * (1.0 + 3.0 * 0.044715 * gb * gb)
        h = hs_ref[...]
        h_first = jnp.where(ti == 0, 0.0, hh_ref[7:8, :])
        h_prev = jnp.where(row == 0, h_first, pltpu.roll(h, 1, 0))
        dy_v = dy_ref[...]
        d_gb = dy_v * h * dgel
        c_c = jnp.where(row < Lt - 1, pltpu.roll(a, Lt - 1, 0), 0.0)
        g_c = dy_v * gel + jnp.where(row == Lt - 1, lam_s[0:1, :], 0.0)
        d = 1
        while d < Lt:
            msk = row < Lt - d
            g_c = jnp.where(msk, c_c * pltpu.roll(g_c, Lt - d, 0) + g_c, g_c)
            c_c = jnp.where(msk, c_c * pltpu.roll(c_c, Lt - d, 0), c_c)
            d *= 2
        lam = g_c
        lam_s[0:1, :] = (a * lam)[0:1, :]
        d_mult = lam * ig * rec_c
        d_loga = lam * h_prev * a - d_mult * (a * a) / mult
        d_ig = lam * mult * rec_c
        d_rec = lam * mult * ig
        d_r = d_loga * (R_C * lsa)
        dap_ref[...] += jnp.sum(d_loga * (R_C * r), axis=0, keepdims=True) * (1.0 - jnp.exp(lsa))
        dgates = jnp.concatenate([d_r * r * (1.0 - r), d_ig * ig * (1.0 - ig)], axis=1)
        dgb_ref[0] += jnp.sum(dgates, axis=0, keepdims=True)
        dgates_b = dgates.astype(BF16)
        dgw_ref[0] += _dot(rec_c.astype(BF16), dgates_b, "tn")
        d_rec = d_rec + _dot(dgates_b, gw_ref[0], "nt")
        dcb_ref[...] += jnp.sum(d_rec, axis=0, keepdims=True)
        dcw_ref[...] += jnp.concatenate(
            [jnp.sum(d_rec * xs[j], axis=0, keepdims=True) for j in range(CONV_W)], axis=0)
        ext = jnp.concatenate([d_rec, drc_s[...]], axis=0)
        up = lambda s: pltpu.roll(ext, Lt + 8 - s, 0)[:Lt]
        d_x = cw[3:4] * d_rec + cw[2:3] * up(1) + cw[1:2] * up(2) + cw[0:1] * up(3)
        drc_s[...] = d_rec[0:8, :]
        dpr_ref[:, :W] = d_gb.astype(BF16)
        dpr_ref[:, W:] = d_x.astype(BF16)

    blk = pl.BlockSpec((Lt, W), lambda g, i: (tmap(i), g))
    return pl.pallas_call(
        body, name="rglru_bwd", grid=(RBLOCKS, nt),
        in_specs=_rglru_specs(T, D, Lt, tmap) + [
            blk,
            pl.BlockSpec((8, W), lambda g, i: (jnp.maximum(tmap(i) * hb - 1, 0), g)),
            blk,
        ],
        out_specs=[
            pl.BlockSpec((Lt, 2 * W), lambda g, i: (tmap(i), g)),
            pl.BlockSpec((CONV_W, W), lambda g, i: (0, g)),
            pl.BlockSpec((1, W), lambda g, i: (0, g)),
            pl.BlockSpec((1, W, 2 * W), lambda g, i: (g, 0, 0)),
            pl.BlockSpec((1, 1, 2 * W), lambda g, i: (g, 0, 0)),
            pl.BlockSpec((1, W), lambda g, i: (0, g)),
        ],
        out_shape=[
            jax.ShapeDtypeStruct((T, 2 * D), BF16),
            jax.ShapeDtypeStruct((CONV_W, D), F32),
            jax.ShapeDtypeStruct((1, D), F32),
            jax.ShapeDtypeStruct((RBLOCKS, W, 2 * W), F32),
            jax.ShapeDtypeStruct((RBLOCKS, 1, 2 * W), F32),
            jax.ShapeDtypeStruct((1, D), F32),
        ],
        scratch_shapes=[pltpu.VMEM((8, W), F32), pltpu.VMEM((8, W), F32)],
        compiler_params=_cparams(("parallel", "arbitrary")),
    )(pr, pr, conv_w, conv_b, gate_w, gate_b, a_param, hs, hs, dy)


def _adamw(name, w, g, m, v):
    R, C = w.shape
    tr = _pick(R, max(8, (1 << 19) // C), 8)

    def body(w_ref, g_ref, m_ref, v_ref, d_ref, nm_ref, nv_ref):
        gv = g_ref[...]
        mv = ADAM_B1 * m_ref[...] + (1.0 - ADAM_B1) * gv
        vv = ADAM_B2 * v_ref[...] + (1.0 - ADAM_B2) * (gv * gv)
        m_hat = mv / (1.0 - ADAM_B1 ** ADAM_STEP)
        v_hat = vv / (1.0 - ADAM_B2 ** ADAM_STEP)
        d_ref[...] = -ADAM_LR * (m_hat / (jnp.sqrt(v_hat) + ADAM_EPS) + ADAM_WD * w_ref[...])
        nm_ref[...] = mv
        nv_ref[...] = vv

    blk = pl.BlockSpec((tr, C), lambda i: (i, 0))
    sds = jax.ShapeDtypeStruct((R, C), F32)
    return pl.pallas_call(
        body, name=name, grid=(R // tr,),
        in_specs=[blk] * 4, out_specs=[blk] * 3, out_shape=[sds] * 3,
        compiler_params=_cparams(("parallel",)),
    )(w, g, m, v)


def _ffn_gran(dff):
    return 256 if dff % 256 == 0 else LANES


def _interleave(w, gran):
    f = w.shape[-1] // 2
    lead = w.shape[:-1]
    return w.reshape(lead + (2, f // gran, gran)).swapaxes(-3, -2).reshape(lead + (2 * f,))


def _deinterleave(w, gran):
    f = w.shape[-1] // 2
    lead = w.shape[:-1]
    return w.reshape(lead + (f // gran, 2, gran)).swapaxes(-3, -2).reshape(lead + (2 * f,))


def _local_step(x, tgt, p):
    T, D = x.shape
    dff = p["ffn_w_out"].shape[1]
    gran = _ffn_gran(dff)
    pw = 3 * D + IF_PAD
    nc = T // CHUNK
    tm = _pick(T, 1024, 128)
    tkt = _pick(T, 1024, 128)
    td = _pick(D, 1024, 128)
    vec = lambda a, l: a[l:l + 1]
    g = {}

    def ffn_fwd(l, h_in):
        hn, r = _rms_fwd(f"ffn{l}_norm", h_in, vec(p["norm_ffn"], l))
        gu, act = _mm(f"ffn{l}_in", hn, p["ffn_w_in"][l], "nn", [(BF16, 2 * dff), (BF16, dff)],
                      tm=tm, tn=_pick(2 * dff, 1024, 2 * gran), tk=D, epi=_make_epi_swiglu(gran))
        (h_out,) = _mm(f"ffn{l}_out", act, p["ffn_w_out"][l], "nn", [(F32, D)],
                       tm=tm, tn=td, tk=_pick(dff, 2816, 128), epi=_epi_resid, extras=(h_in,))
        return h_out, (hn, r, gu, act)

    def ffn_bwd(l, h_in, saved, dh, dhb):
        hn, r, gu, act = saved
        (dgu,) = _mm(f"ffn{l}_dact", dhb, p["ffn_w_out"][l], "nt", [(BF16, 2 * dff)],
                     tm=tm, tn=_pick(dff, 512, gran), tk=D, epi=_make_epi_swiglu_bwd(gran), extras=(gu,))
        (dw_out,) = _mm(f"ffn{l}_dwout", act, dhb, "tn", [(BF16, D)],
                        tm=_pick(dff, 1024, 128), tn=td, tk=tkt)
        (dw_in,) = _mm(f"ffn{l}_dwin", hn, dgu, "tn", [(BF16, 2 * dff)],
                       tm=td, tn=_pick(2 * dff, 1024, 128), tk=tkt)
        (dhn,) = _mm(f"ffn{l}_dx", dgu, p["ffn_w_in"][l], "nt", [(F32, D)],
                     tm=tm, tn=td, tk=_pick(2 * dff, 2048, 128))
        dh, dhb, dg = _rms_bwd(f"ffn{l}_dnorm", h_in, r, dhn, vec(p["norm_ffn"], l), dh)
        return dh, dhb, dw_in, dw_out, dg

    h0 = x
    hn0, r0 = _rms_fwd("mix0_norm", h0, vec(p["norm_mix"], 0))
    (proj,) = _mm("m_proj", hn0, p["m_w_in"], "nn", [(F32, pw)], tm=tm, tn=_pick(pw, 1024, 128), tk=D)
    gcol = _mlstm_gates(proj, p["m_b_if"], D)
    grow = gcol[:, :3 * HEADS].reshape(nc, CHUNK, 3 * HEADS).transpose(0, 2, 1)
    hg, cs, ns, ms = _mlstm_fwd(proj, gcol, grow, p["m_head_norm"])
    (h1,) = _mm("m_out", hg, p["m_w_out"], "nn", [(F32, D)], tm=tm, tn=td, tk=D, epi=_epi_resid, extras=(h0,))
    h2, ffn0 = ffn_fwd(0, h1)
    hn2, r2 = _rms_fwd("mix1_norm", h2, vec(p["norm_mix"], 1))
    (pr,) = _mm("r_proj", hn2, p["r_w_in"], "nn", [(F32, 2 * D)], tm=tm, tn=td, tk=D)
    y, hs = _rglru_fwd(pr, p["conv_w"], p["conv_b"], p["gate_w"], p["gate_b"], p["a_param"])
    (h3,) = _mm("r_out", y, p["r_w_out"], "nn", [(F32, D)], tm=tm, tn=td, tk=D, epi=_epi_resid, extras=(h2,))
    h4, ffn1 = ffn_fwd(1, h3)
    loss, dh, dhb, g["norm_final"] = _loss_head(h4, p["norm_final"], tgt)

    dh, dhb, dwi1, dwo1, dnf1 = ffn_bwd(1, h3, ffn1, dh, dhb)
    (dy,) = _mm("r_dy", dhb, p["r_w_out"], "nt", [(F32, D)], tm=tm, tn=td, tk=D)
    (g["r_w_out"],) = _mm("r_dwout", y, dhb, "tn", [(BF16, D)], tm=td, tn=td, tk=tkt)
    dpr, g["conv_w"], g["conv_b"], g["gate_w"], g["gate_b"], g["a_param"] = _rglru_bwd(
        pr, hs, dy, p["conv_w"], p["conv_b"], p["gate_w"], p["gate_b"], p["a_param"])
    (g["r_w_in"],) = _mm("r_dwin", hn2, dpr, "tn", [(BF16, 2 * D)], tm=td, tn=td, tk=tkt)
    (dhn2,) = _mm("r_dx", dpr, p["r_w_in"], "nt", [(F32, D)], tm=tm, tn=td, tk=_pick(2 * D, 2048, 128))
    dh, dhb, dnm1 = _rms_bwd("mix1_dnorm", h2, r2, dhn2, vec(p["norm_mix"], 1), dh)
    dh, dhb, dwi0, dwo0, dnf0 = ffn_bwd(0, h1, ffn0, dh, dhb)
    (dhg,) = _mm("m_dhg", dhb, p["m_w_out"], "nt", [(F32, D)], tm=tm, tn=td, tk=D)
    (g["m_w_out"],) = _mm("m_dwout", hg, dhb, "tn", [(BF16, D)], tm=td, tn=td, tk=tkt)
    dproj, g["m_head_norm"], g["m_b_if"] = _mlstm_bwd(proj, gcol, grow, p["m_head_norm"], dhg, cs, ns, ms)
    (g["m_w_in"],) = _mm("m_dwin", hn0, dproj, "tn", [(BF16, pw)], tm=td, tn=_pick(pw, 1024, 128), tk=tkt)
    (dhn0,) = _mm("m_dx", dproj, p["m_w_in"], "nt", [(F32, D)], tm=tm, tn=td, tk=_pick(pw, 1024, 128))
    grad_x, _, dnm0 = _rms_bwd("mix0_dnorm", h0, r0, dhn0, vec(p["norm_mix"], 0), dh)
    g["norm_mix"] = jnp.concatenate([dnm0, dnm1], axis=0)
    g["norm_ffn"] = jnp.concatenate([dnf0, dnf1], axis=0)
    g["ffn_w_in"] = jnp.stack([dwi0, dwi1])
    g["ffn_w_out"] = jnp.stack([dwo0, dwo1])
    return loss, grad_x, g


_MESH = pl.DeviceIdType.MESH
_HBM = pl.BlockSpec(memory_space=pltpu.HBM)
N_CHIPS = 4


def _place():
    x, y, c = lax.axis_index("x"), lax.axis_index("y"), lax.axis_index("c")
    chips = [(1 - x, y), (x, 1 - y), (1 - x, 1 - y)]
    return x, y, c, 2 * x + y, chips, [2 * px + py for px, py in chips]


def _allgather(arrs):
    n = len(arrs)

    def body(*refs):
        ins, outs = refs[:n], refs[n:2 * n]
        loc_sems, send_sems, recv_sems, fsend_sems, frecv_sems = refs[2 * n:]
        x, y, c, j, chips, cj = _place()
        sib = (x, y, 1 - c)

        def half(w, cc):
            hw = arrs[w].shape[0] // 2
            return pl.ds(cc * hw, hw)

        def ici(w, k, region):
            return pltpu.make_async_remote_copy(
                src_ref=ins[w].at[half(w, c)], dst_ref=region, send_sem=send_sems.at[w, k],
                recv_sem=recv_sems.at[w, k], device_id=(*chips[k], c), device_id_type=_MESH)

        def d2d(w, k, region):
            return pltpu.make_async_remote_copy(
                src_ref=region, dst_ref=region, send_sem=fsend_sems.at[w, k],
                recv_sem=frecv_sems.at[w, k], device_id=sib, device_id_type=_MESH)

        locs = [pltpu.make_async_copy(ins[w], outs[w].at[j], loc_sems.at[w]) for w in range(n)]
        for cp in locs:
            cp.start()
        sends = [ici(w, k, outs[w].at[j, half(w, c)]) for w in range(n) for k in range(3)]
        for cp in sends:
            cp.start()
        fwds = []
        for w in range(n):
            for k in range(3):
                region = outs[w].at[cj[k], half(w, c)]
                ici(w, k, region).wait_recv()
                fwds.append(d2d(w, k, region))
                fwds[-1].start()
        for w in range(n):
            for k in range(3):
                d2d(w, k, outs[w].at[cj[k], half(w, 1 - c)]).wait_recv()
        for cp in sends + fwds:
            cp.wait_send()
        for cp in locs:
            cp.wait()

    return pl.pallas_call(
        body, name="allgather_weights",
        in_specs=[_HBM] * n, out_specs=[_HBM] * n,
        out_shape=[jax.ShapeDtypeStruct((N_CHIPS,) + a.shape, a.dtype) for a in arrs],
        scratch_shapes=[pltpu.SemaphoreType.DMA((n,))] + [pltpu.SemaphoreType.DMA((n, 3))] * 4,
    )(*arrs)


def _rs_to_sibling(gps):
    n = len(gps)

    def body(*refs):
        ins, outs = refs[:n], refs[n:2 * n]
        send_sems, recv_sems = refs[2 * n:]
        x, y, c, _, _, _ = _place()
        cps = []
        for w in range(n):
            hw = gps[w].shape[1] // 2
            cps.append(pltpu.make_async_remote_copy(
                src_ref=ins[w].at[:, pl.ds((1 - c) * hw, hw)], dst_ref=outs[w], send_sem=send_sems.at[w],
                recv_sem=recv_sems.at[w], device_id=(x, y, 1 - c), device_id_type=_MESH))
            cps[-1].start()
        for cp in cps:
            cp.wait()

    return pl.pallas_call(
        body, name="rs_to_sibling",
        in_specs=[_HBM] * n, out_specs=[_HBM] * n,
        out_shape=[jax.ShapeDtypeStruct((N_CHIPS, a.shape[1] // 2, a.shape[2]), a.dtype) for a in gps],
        scratch_shapes=[pltpu.SemaphoreType.DMA((n,))] * 2,
    )(*gps)


def _rs_add_pair(name, gp, sib, c_idx):
    _, R, C = gp.shape
    hr = R // 2
    br = _pick(hr, max(16, (1 << 19) // C), 16)
    nb = hr // br

    def body(c_ref, g_ref, s_ref, o_ref):
        o_ref[...] = (g_ref[...].astype(F32) + s_ref[...].astype(F32)).astype(o_ref.dtype)

    blk = pl.BlockSpec((1, br, C), lambda s, i, c_ref: (s, i, 0))
    return pl.pallas_call(
        body, name=name,
        grid_spec=pltpu.PrefetchScalarGridSpec(
            num_scalar_prefetch=1, grid=(N_CHIPS, nb),
            in_specs=[pl.BlockSpec((1, br, C), lambda s, i, c_ref: (s, c_ref[0] * nb + i, 0)), blk],
            out_specs=blk),
        out_shape=jax.ShapeDtypeStruct((N_CHIPS, hr, C), gp.dtype),
        compiler_params=_cparams(("parallel", "parallel")),
    )(c_idx, gp, sib)


def _rs_to_chips(ps):
    n = len(ps)

    def body(*refs):
        ins, outs = refs[:n], refs[n:2 * n]
        send_sems, recv_sems = refs[2 * n:]
        _, _, c, _, chips, cj = _place()
        cps = []
        for w in range(n):
            for k in range(3):
                cps.append(pltpu.make_async_remote_copy(
                    src_ref=ins[w].at[cj[k]], dst_ref=outs[w].at[k], send_sem=send_sems.at[w, k],
                    recv_sem=recv_sems.at[w, k], device_id=(*chips[k], c), device_id_type=_MESH))
                cps[-1].start()
        for cp in cps:
            cp.wait()

    return pl.pallas_call(
        body, name="rs_to_chips",
        in_specs=[_HBM] * n, out_specs=[_HBM] * n,
        out_shape=[jax.ShapeDtypeStruct((3,) + a.shape[1:], a.dtype) for a in ps],
        scratch_shapes=[pltpu.SemaphoreType.DMA((n, 3))] * 2,
    )(*ps)


def _rs_add_chips(name, p, b, jc_idx):
    _, hr, C = p.shape
    br = _pick(hr, max(16, (1 << 18) // C), 16)

    def body(jc_ref, p_ref, b_ref, o_ref):
        o_ref[0] = p_ref[0].astype(F32) + b_ref[0].astype(F32) + b_ref[1].astype(F32) + b_ref[2].astype(F32)

    return pl.pallas_call(
        body, name=name,
        grid_spec=pltpu.PrefetchScalarGridSpec(
            num_scalar_prefetch=1, grid=(hr // br,),
            in_specs=[pl.BlockSpec((1, br, C), lambda i, jc: (jc[0], i, 0)),
                      pl.BlockSpec((3, br, C), lambda i, jc: (0, i, 0))],
            out_specs=pl.BlockSpec((1, br, C), lambda i, jc: (jc[1], i, 0))),
        out_shape=jax.ShapeDtypeStruct((2, hr, C), F32),
        compiler_params=_cparams(("parallel",)),
    )(jc_idx, p, b)


def _rs_share_halves(rs):
    n = len(rs)

    def body(*refs):
        outs = refs[n:2 * n]
        send_sems, recv_sems = refs[2 * n:]
        x, y, c, _, _, _ = _place()
        cps = []
        for w in range(n):
            cps.append(pltpu.make_async_remote_copy(
                src_ref=outs[w].at[c], dst_ref=outs[w].at[c], send_sem=send_sems.at[w],
                recv_sem=recv_sems.at[w], device_id=(x, y, 1 - c), device_id_type=_MESH))
            cps[-1].start()
        for w in range(n):
            cps[w].wait_send()
            pltpu.make_async_remote_copy(
                src_ref=outs[w].at[1 - c], dst_ref=outs[w].at[1 - c], send_sem=send_sems.at[w],
                recv_sem=recv_sems.at[w], device_id=(x, y, 1 - c), device_id_type=_MESH).wait_recv()

    return pl.pallas_call(
        body, name="rs_share_halves",
        in_specs=[_HBM] * n, out_specs=[_HBM] * n,
        out_shape=[jax.ShapeDtypeStruct(a.shape, a.dtype) for a in rs],
        input_output_aliases={w: w for w in range(n)},
        scratch_shapes=[pltpu.SemaphoreType.DMA((n,))] * 2,
    )(*rs)


def _allreduce_small(v):
    R = v.shape[0]

    def body(v_ref, o_ref, slots, send_sems, recv_sems):
        x, y, c = lax.axis_index("x"), lax.axis_index("y"), lax.axis_index("c")
        me = 4 * x + 2 * y + c
        cps = []
        for k in range(1, 8):
            peer = ((1 - x) if k & 4 else x, (1 - y) if k & 2 else y, (1 - c) if k & 1 else c)
            cps.append(pltpu.make_async_remote_copy(
                src_ref=v_ref, dst_ref=slots.at[k - 1], send_sem=send_sems.at[k - 1],
                recv_sem=recv_sems.at[k - 1], device_id=peer, device_id_type=_MESH))
            cps[-1].start()
        for cp in cps:
            cp.wait()
        acc = jnp.zeros((R, LANES), F32)
        for d in range(8):
            k = ((d // 4) ^ x) * 4 + (((d // 2) % 2) ^ y) * 2 + ((d % 2) ^ c)
            other = slots[jnp.maximum(k - 1, 0)]
            acc = acc + jnp.where(d == me, v_ref[...], other)
        o_ref[...] = acc

    vm = pl.BlockSpec(memory_space=pltpu.VMEM)
    return pl.pallas_call(
        body, name="allreduce_small",
        in_specs=[vm], out_specs=vm,
        out_shape=jax.ShapeDtypeStruct((R, LANES), F32),
        scratch_shapes=[pltpu.VMEM((7, R, LANES), F32), pltpu.SemaphoreType.DMA((7,)), pltpu.SemaphoreType.DMA((7,))],
    )(v)


def _reduce_scatter(gps):
    x, y, c = lax.axis_index("x"), lax.axis_index("y"), lax.axis_index("c")
    c_idx = jnp.reshape(c, (1,)).astype(jnp.int32)
    jc_idx = jnp.stack([2 * x + y, c]).astype(jnp.int32)
    sibs = _rs_to_sibling(gps)
    ps = [_rs_add_pair(f"rs_add_pair{w}", gp, sb, c_idx) for w, (gp, sb) in enumerate(zip(gps, sibs))]
    bs = _rs_to_chips(ps)
    rs = [_rs_add_chips(f"rs_add_chips{w}", p, b, jc_idx) for w, (p, b) in enumerate(zip(ps, bs))]
    rs = _rs_share_halves(rs)
    return [r.reshape(gp.shape[1:]) for r, gp in zip(rs, gps)]


def _rows(a):
    flat = a.reshape(-1)
    return jnp.pad(flat, (0, (-flat.shape[0]) % LANES)).reshape(-1, LANES)


def _pack_rows(parts, mult=8):
    v = jnp.concatenate([_rows(a) for a in parts], axis=0)
    return jnp.pad(v, ((0, (-v.shape[0]) % mult), (0, 0)))


def _unpack_rows(v, shapes):
    out, r = [], 0
    for s in shapes:
        size = math.prod(s)
        nr = -(-size // LANES)
        out.append(v[r:r + nr].reshape(-1)[:size].reshape(s))
        r += nr
    return out


def kernel(x, norm_mix, norm_ffn, norm_final, m_w_in, m_b_if, m_head_norm, m_w_out, r_w_in, r_conv_w, r_conv_b, r_gate_w, r_gate_b, r_a_param, r_w_out, ffn_w_in, ffn_w_out, loss_target, m_norm_mix, m_norm_ffn, m_norm_final, m_m_w_in, m_m_b_if, m_m_head_norm, m_m_w_out, m_r_w_in, m_r_conv_w, m_r_conv_b, m_r_gate_w, m_r_gate_b, m_r_a_param, m_r_w_out, m_ffn_w_in, m_ffn_w_out, v_norm_mix, v_norm_ffn, v_norm_final, v_m_w_in, v_m_b_if, v_m_head_norm, v_m_w_out, v_r_w_in, v_r_conv_w, v_r_conv_b, v_r_gate_w, v_r_gate_b, v_r_a_param, v_r_w_out, v_ffn_w_in, v_ffn_w_out):
    names = ["norm_mix", "norm_ffn", "norm_final", "m_w_in", "m_b_if", "m_head_norm", "m_w_out", "r_w_in", "r_conv_w",
             "r_conv_b", "r_gate_w", "r_gate_b", "r_a_param", "r_w_out", "ffn_w_in", "ffn_w_out"]
    w = dict(zip(names, [norm_mix, norm_ffn, norm_final, m_w_in, m_b_if, m_head_norm, m_w_out, r_w_in, r_conv_w,
                         r_conv_b, r_gate_w, r_gate_b, r_a_param, r_w_out, ffn_w_in, ffn_w_out]))
    mom = dict(zip(names, [m_norm_mix, m_norm_ffn, m_norm_final, m_m_w_in, m_m_b_if, m_m_head_norm, m_m_w_out, m_r_w_in,
                           m_r_conv_w, m_r_conv_b, m_r_gate_w, m_r_gate_b, m_r_a_param, m_r_w_out, m_ffn_w_in, m_ffn_w_out]))
    var = dict(zip(names, [v_norm_mix, v_norm_ffn, v_norm_final, v_m_w_in, v_m_b_if, v_m_head_norm, v_m_w_out, v_r_w_in,
                           v_r_conv_w, v_r_conv_b, v_r_gate_w, v_r_gate_b, v_r_a_param, v_r_w_out, v_ffn_w_in, v_ffn_w_out]))
    D = x.shape[-1]
    W = D // RBLOCKS
    dff = ffn_w_out.shape[1] * N_CHIPS
    gran = _ffn_gran(dff)
    nproj = m_w_in.shape[2] * N_CHIPS
    shard_col = lax.axis_index("x") * 2 + lax.axis_index("y")

    big = ["m_w_in", "m_w_out", "r_w_in", "r_gate_w", "r_w_out", "ffn_w_in", "ffn_w_out"]
    shard2d = {n: w[n].reshape(-1, w[n].shape[-1]) for n in big}
    small_sharded = ["r_conv_w", "r_conv_b", "r_gate_b", "r_a_param"]
    small_pack = _pack_rows([w[n] for n in small_sharded], mult=16)
    gathered = _allgather([shard2d[n].astype(BF16) for n in big] + [small_pack])
    gw = dict(zip(big, gathered[:-1]))
    sp = gathered[-1]
    ws = D // N_CHIPS
    wg = 2 * W // N_CHIPS
    r1, r2, r3 = CONV_W * ws // LANES, (CONV_W + 1) * ws // LANES, ((CONV_W + 1) * ws + RBLOCKS * wg) // LANES
    conv_w_f = sp[:, :r1].reshape(N_CHIPS, CONV_W, ws).transpose(1, 0, 2).reshape(CONV_W, D)
    conv_b_f = sp[:, r1:r2].reshape(1, D)
    gate_b_f = sp[:, r2:r3].reshape(N_CHIPS, RBLOCKS, wg).transpose(1, 0, 2).reshape(RBLOCKS, 1, 2 * W)
    a_param_f = sp[:, r3:r3 + ws // LANES].reshape(1, D)

    def cols(g, lead):
        nl = len(lead)
        g = g.reshape((N_CHIPS,) + lead + (-1,))
        g = jnp.moveaxis(g, 0, nl)
        return g.reshape(lead + (-1,))

    p = dict(
        norm_mix=norm_mix, norm_ffn=norm_ffn, norm_final=norm_final.reshape(1, D),
        m_b_if=jnp.pad(m_b_if, ((0, 0), (0, IF_PAD - m_b_if.shape[1]))), m_head_norm=m_head_norm,
        conv_w=conv_w_f, conv_b=conv_b_f, gate_b=gate_b_f, a_param=a_param_f,
        m_w_in=jnp.pad(cols(gw["m_w_in"], (D,)), ((0, 0), (0, 3 * D + IF_PAD - nproj))),
        m_w_out=gw["m_w_out"].reshape(D, D),
        r_w_in=_interleave(cols(gw["r_w_in"], (D,)), W),
        gate_w=cols(gw["r_gate_w"], (RBLOCKS, W)),
        r_w_out=gw["r_w_out"].reshape(D, D),
        ffn_w_in=_interleave(cols(gw["ffn_w_in"], (2, D)), gran),
        ffn_w_out=jnp.moveaxis(gw["ffn_w_out"].reshape(N_CHIPS, 2, dff // N_CHIPS, D), 0, 1).reshape(2, dff, D),
    )

    loss_part, grad_x, g = _local_step(x[0], loss_target[0], p)

    def uncols(a, lead):
        nl = len(lead)
        a = a.reshape(lead + (N_CHIPS, -1))
        a = jnp.moveaxis(a, nl, 0)
        return a.reshape(N_CHIPS, math.prod(lead), -1)

    gps = [
        uncols(g["m_w_in"][:, :nproj], (D,)),
        g["m_w_out"].reshape(N_CHIPS, D // N_CHIPS, D),
        uncols(_deinterleave(g["r_w_in"], W), (D,)),
        uncols(g["gate_w"].astype(BF16), (RBLOCKS, W)),
        g["r_w_out"].reshape(N_CHIPS, D // N_CHIPS, D),
        uncols(_deinterleave(g["ffn_w_in"], gran), (2, D)),
        jnp.moveaxis(g["ffn_w_out"].reshape(2, N_CHIPS, dff // N_CHIPS, D), 1, 0).reshape(N_CHIPS, 2 * dff // N_CHIPS, D),
    ]
    reduced = dict(zip(big, _reduce_scatter(gps)))

    small_all = ["norm_mix", "norm_ffn", "norm_final", "m_b_if", "m_head_norm", "r_conv_w", "r_conv_b", "r_gate_b", "r_a_param"]
    gsmall = [g["norm_mix"], g["norm_ffn"], g["norm_final"], g["m_b_if"], g["m_head_norm"], g["conv_w"], g["conv_b"],
              g["gate_b"].reshape(RBLOCKS, 2 * W), g["a_param"], loss_part]
    summed = _unpack_rows(_allreduce_small(_pack_rows(gsmall)), [a.shape for a in gsmall])
    loss = summed[-1][0, 0]
    gs = dict(zip(small_all, summed[:-1]))
    gs["norm_final"] = gs["norm_final"].reshape(D)
    gs["m_b_if"] = gs["m_b_if"][:, :m_b_if.shape[1]]
    gs["r_conv_w"] = lax.dynamic_slice_in_dim(gs["r_conv_w"], shard_col * ws, ws, axis=1).reshape(r_conv_w.shape)
    gs["r_conv_b"] = lax.dynamic_slice_in_dim(gs["r_conv_b"], shard_col * ws, ws, axis=1).reshape(r_conv_b.shape)
    gs["r_gate_b"] = lax.dynamic_slice_in_dim(gs["r_gate_b"], shard_col * (2 * W // N_CHIPS), 2 * W // N_CHIPS,
                                              axis=1).reshape(r_gate_b.shape)
    gs["r_a_param"] = lax.dynamic_slice_in_dim(gs["r_a_param"], shard_col * ws, ws, axis=1).reshape(r_a_param.shape)

    grads, delta, new_m, new_v = {}, {}, {}, {}
    for n in big:
        shp = w[n].shape
        grads[n] = reduced[n].reshape(shp)
        d_, m_, v_ = _adamw(f"adamw_{n}", shard2d[n], reduced[n], mom[n].reshape(shard2d[n].shape),
                            var[n].reshape(shard2d[n].shape))
        delta[n], new_m[n], new_v[n] = d_.reshape(shp), m_.reshape(shp), v_.reshape(shp)

    packs = [_pack_rows([t[n] for n in small_all]) for t in (w, gs, mom, var)]
    outs = _adamw("adamw_small", *packs)
    for dst, pk in zip((delta, new_m, new_v), outs):
        for n, a in zip(small_all, _unpack_rows(pk, [w[n].shape for n in small_all])):
            dst[n] = a
    for n in small_all:
        grads[n] = gs[n]
    return (loss, grad_x[None], *[grads[n] for n in names], *[delta[n] for n in names],
            *[new_m[n] for n in names], *[new_v[n] for n in names])
```

```python
import functools
import math

import jax
import jax.numpy as jnp
from jax import lax
from jax.experimental import pallas as pl
from jax.experimental.pallas import tpu as pltpu

F32 = jnp.float32
BF16 = jnp.bfloat16

EPS = 1e-6
HEADS = 8
CHUNK = 64
RBLOCKS = 8
CONV_W = 4
R_C = 8.0
LANES = 128
IF_PAD = LANES
VMEM_LIMIT = 52 << 20

ADAM_LR = 0.001
ADAM_B1 = 0.9
ADAM_B2 = 0.999
ADAM_EPS = 1e-08
ADAM_WD = 0.01
ADAM_STEP = 10

_DIMS = {
    "nn": (((1,), (0,)), ((), ())),
    "nt": (((1,), (1,)), ((), ())),
    "tn": (((0,), (0,)), ((), ())),
}


def _dot(a, b, mode="nn"):
    return lax.dot_general(a, b, _DIMS[mode], preferred_element_type=F32)


def _pick(n, pref, mult):
    best = None
    d = mult
    while d <= min(n, pref):
        if n % d == 0:
            best = d
        d += mult
    return best if best is not None else n


def _cparams(sem):
    return pltpu.CompilerParams(dimension_semantics=sem, vmem_limit_bytes=VMEM_LIMIT)


def _a_plain(a, mode, tm, tk):
    if mode == "tn":
        return [(a, (tk, tm), lambda i, j, k: (k, i))], lambda t, i, j, k: t[0]
    return [(a, (tm, tk), lambda i, j, k: (i, k))], lambda t, i, j, k: t[0]


def _a_split_k(a1, a2, tm, tk):
    h = a1.shape[1] // tk
    return ([(a1, (tm, tk), lambda i, j, k: (i, jnp.minimum(k, h - 1))),
             (a2, (tm, tk), lambda i, j, k: (i, jnp.maximum(k - h, 0)))],
            lambda t, i, j, k: jnp.where(k < h, t[0], t[1]))


def _sigmoid(x):
    return 1.0 / (1.0 + jnp.exp(-x))


def _a_swiglu(gu, mode, tm, tk):
    f = gu.shape[1] // 2

    def comb(t, i, j, k):
        g = t[0].astype(F32)
        return (g * _sigmoid(g) * t[1].astype(F32)).astype(BF16)

    if mode == "tn":
        off = f // tm
        return [(gu, (tk, tm), lambda i, j, k: (k, i)), (gu, (tk, tm), lambda i, j, k: (k, i + off))], comb
    off = f // tk
    return [(gu, (tm, tk), lambda i, j, k: (i, k)), (gu, (tm, tk), lambda i, j, k: (i, k + off))], comb


def _b_plain(b, mode, tn, tk):
    if mode == "nt":
        return [(b, (tn, tk), lambda i, j, k: (j, k))], lambda t, i, j, k: t[0]
    return [(b, (tk, tn), lambda i, j, k: (k, j))], lambda t, i, j, k: t[0]


def _b_split_n(b1, b2, tn, tk):
    h = b1.shape[1] // tn
    return ([(b1, (tk, tn), lambda i, j, k: (k, jnp.minimum(j, h - 1))),
             (b2, (tk, tn), lambda i, j, k: (k, jnp.maximum(j - h, 0)))],
            lambda t, i, j, k: jnp.where(j < h, t[0], t[1]))


def _mm(name, a_op, b_op, mode, M, N, K, out_defs, *, tm, tn, tk, epi=None, extras=(), carry=None):
    assert M % tm == 0 and N % tn == 0 and K % tk == 0, (name, M, N, K, tm, tn, tk)
    (a_in, a_fn), (b_in, b_fn) = a_op, b_op
    ni, nj, nk = M // tm, N // tn, K // tk
    na, nb, ne, no = len(a_in), len(b_in), len(extras), len(out_defs)
    nci = len(carry.ins) if carry else 0
    nco = len(carry.out_shapes) if carry else 0
    ncs = len(carry.scratch) if carry else 0

    def body(*refs):
        pos = 0

        def take(n):
            nonlocal pos
            r = refs[pos:pos + n]
            pos += n
            return r

        a_refs, b_refs, ex, c_in, outs, c_out = take(na), take(nb), take(ne), take(nci), take(no), take(nco)
        acc_ref = take(1)[0] if nk > 1 else None
        c_scr = take(ncs)
        i, j, k = pl.program_id(0), pl.program_id(1), pl.program_id(2)
        if carry:
            @pl.when((i == 0) & (j == 0) & (k == 0))
            def _():
                carry.start(c_in, c_out, c_scr)

        def finish(acc):
            res = epi(acc, *[e[...] for e in ex]) if epi is not None else (acc,)
            for o_ref, r in zip(outs, res):
                o_ref[...] = r.astype(o_ref.dtype)

        av = a_fn([r[...] for r in a_refs], i, j, k).astype(BF16)
        bv = b_fn([r[...] for r in b_refs], i, j, k).astype(BF16)
        d = _dot(av, bv, mode)
        if nk == 1:
            finish(d)
        else:
            @pl.when(k == 0)
            def _():
                acc_ref[...] = d

            @pl.when(k > 0)
            def _():
                acc_ref[...] += d

            @pl.when(k == nk - 1)
            def _():
                finish(acc_ref[...])
        if carry:
            @pl.when((i == ni - 1) & (j == nj - 1) & (k == nk - 1))
            def _():
                carry.finish(c_in, c_out, c_scr)

    def spec(shape, imap):
        return pl.BlockSpec(shape, imap)

    in_specs = [spec(s, m) for _, s, m in a_in + b_in]
    in_specs += [pl.BlockSpec((tm, w), lambda i, j, k, off=off: (i, j + off)) for _, w, off in extras]
    out_specs = [pl.BlockSpec((tm, c * tn // N), lambda i, j, k: (i, j)) for _, c in out_defs]
    out_shape = [jax.ShapeDtypeStruct((M, c), dt) for dt, c in out_defs]
    scratch = [pltpu.VMEM((tm, tn), F32)] if nk > 1 else []
    operands = [x[0] for x in a_in + b_in] + [e[0] for e in extras]
    if carry:
        in_specs += [_HBM] * nci
        out_specs += [_HBM] * nco
        out_shape += list(carry.out_shapes)
        scratch += list(carry.scratch)
        operands += list(carry.ins)
    sem = ("arbitrary",) * 3 if carry else ("parallel", "parallel", "arbitrary")
    res = pl.pallas_call(
        body, name=name, grid=(ni, nj, nk), in_specs=in_specs, out_specs=out_specs, out_shape=out_shape,
        scratch_shapes=scratch, compiler_params=_cparams(sem),
    )(*operands)
    return res[:no], res[no:]


def _epi_resid(acc, resid):
    return (resid + acc,)


def _epi_swiglu_bwd(da, g, u):
    g = g.astype(F32)
    u = u.astype(F32)
    sig = _sigmoid(g)
    return da * u * (sig * (1.0 + g * (1.0 - sig))), da * (g * sig)


def _rms_fwd(name, x, g):
    T, D = x.shape
    tr = _pick(T, 512, 16)

    def body(x_ref, g_ref, hn_ref, r_ref):
        xv = x_ref[...]
        r = lax.rsqrt(jnp.mean(xv * xv, axis=-1, keepdims=True) + EPS)
        hn_ref[...] = (xv * r * g_ref[...]).astype(BF16)
        r_ref[...] = r

    return pl.pallas_call(
        body, name=name, grid=(T // tr,),
        in_specs=[pl.BlockSpec((tr, D), lambda i: (i, 0)), pl.BlockSpec((1, D), lambda i: (0, 0))],
        out_specs=[pl.BlockSpec((tr, D), lambda i: (i, 0)), pl.BlockSpec((tr, 1), lambda i: (i, 0))],
        out_shape=[jax.ShapeDtypeStruct((T, D), BF16), jax.ShapeDtypeStruct((T, 1), F32)],
        compiler_params=_cparams(("parallel",)),
    )(x, g)


def _rms_bwd(name, x, r, dy, g, dres):
    T, D = x.shape
    tr = _pick(T, 512, 16)

    def body(x_ref, r_ref, dy_ref, g_ref, dres_ref, dh_ref, dhb_ref, dg_ref):
        xh = x_ref[...] * r_ref[...]
        dy_v = dy_ref[...]
        dyg = dy_v * g_ref[...]
        m = jnp.mean(dyg * xh, axis=-1, keepdims=True)
        dh = dres_ref[...] + r_ref[...] * (dyg - xh * m)
        dh_ref[...] = dh
        dhb_ref[...] = dh.astype(BF16)

        @pl.when(pl.program_id(0) == 0)
        def _():
            dg_ref[...] = jnp.zeros_like(dg_ref)

        dg_ref[...] += jnp.sum(dy_v * xh, axis=0, keepdims=True)

    row = pl.BlockSpec((tr, D), lambda i: (i, 0))
    vec = pl.BlockSpec((1, D), lambda i: (0, 0))
    return pl.pallas_call(
        body, name=name, grid=(T // tr,),
        in_specs=[row, pl.BlockSpec((tr, 1), lambda i: (i, 0)), row, vec, row],
        out_specs=[row, row, vec],
        out_shape=[jax.ShapeDtypeStruct((T, D), F32), jax.ShapeDtypeStruct((T, D), BF16),
                   jax.ShapeDtypeStruct((1, D), F32)],
        compiler_params=_cparams(("arbitrary",)),
    )(x, r, dy, g, dres)


def _loss_head(x, g, tgt):
    T, D = x.shape
    tr = _pick(T, 512, 16)

    def body(x_ref, g_ref, t_ref, loss_ref, dh_ref, dhb_ref, dg_ref):
        xv = x_ref[...]
        gv = g_ref[...]
        r = lax.rsqrt(jnp.mean(xv * xv, axis=-1, keepdims=True) + EPS)
        xh = xv * r
        e = xh * gv - t_ref[...]
        part = 0.5 * jnp.sum(jnp.mean(e * e, axis=-1, keepdims=True), axis=0, keepdims=True)
        dy = e * (1.0 / D)
        dyg = dy * gv
        m = jnp.mean(dyg * xh, axis=-1, keepdims=True)
        dh = r * (dyg - xh * m)
        dh_ref[...] = dh
        dhb_ref[...] = dh.astype(BF16)

        @pl.when(pl.program_id(0) == 0)
        def _():
            dg_ref[...] = jnp.zeros_like(dg_ref)
            loss_ref[...] = jnp.zeros_like(loss_ref)

        dg_ref[...] += jnp.sum(dy * xh, axis=0, keepdims=True)
        loss_ref[...] += jnp.broadcast_to(part, loss_ref.shape)

    row = pl.BlockSpec((tr, D), lambda i: (i, 0))
    vec = pl.BlockSpec((1, D), lambda i: (0, 0))
    return pl.pallas_call(
        body, name="loss_head", grid=(T // tr,),
        in_specs=[row, vec, row],
        out_specs=[pl.BlockSpec((1, LANES), lambda i: (0, 0)), row, row, vec],
        out_shape=[jax.ShapeDtypeStruct((1, LANES), F32), jax.ShapeDtypeStruct((T, D), F32),
                   jax.ShapeDtypeStruct((T, D), BF16), jax.ShapeDtypeStruct((1, D), F32)],
        compiler_params=_cparams(("arbitrary",)),
    )(x, g, tgt)


def _mlstm_gates(proj, b_if_pad, D):
    T = proj.shape[0]
    tr = _pick(T, 512, CHUNK)

    def body(p_ref, b_ref, o_ref):
        z = p_ref[...] + b_ref[...]
        lane = lax.broadcasted_iota(jnp.int32, z.shape, 1)
        row = lax.broadcasted_iota(jnp.int32, z.shape, 0) % CHUNK
        lf = jnp.where((lane >= HEADS) & (lane < 2 * HEADS),
                       jnp.minimum(z, 0.0) - jnp.log(1.0 + jnp.exp(-jnp.abs(z))), 0.0)
        c = lf
        d = 1
        while d < CHUNK:
            c = c + jnp.where(row >= d, pltpu.roll(c, d, 0), 0.0)
            d *= 2
        o_ref[...] = jnp.where(lane < HEADS, z, 0.0) + lf + pltpu.roll(c, HEADS, 1)

    return pl.pallas_call(
        body, name="mlstm_gates", grid=(T // tr,),
        in_specs=[pl.BlockSpec((tr, IF_PAD), lambda i: (i, 3 * D // IF_PAD)),
                  pl.BlockSpec((1, IF_PAD), lambda i: (0, 0))],
        out_specs=pl.BlockSpec((tr, IF_PAD), lambda i: (i, 0)),
        out_shape=jax.ShapeDtypeStruct((T, IF_PAD), F32),
        compiler_params=_cparams(("parallel",)),
    )(proj, b_if_pad)


def _mlstm_chunk_common(qf, kf, vf, gc, gr, h, c_prev, n_prev, m_prev, causal):
    L = CHUNK
    qb = qf.astype(BF16)
    kb = kf.astype(BF16)
    vb = vf.astype(BF16)
    i_col = gc[:, h:h + 1]
    b_col = gc[:, 2 * HEADS + h:2 * HEADS + h + 1]
    i_row = gr[h:h + 1, :]
    b_row = gr[2 * HEADS + h:2 * HEADS + h + 1, :]
    dmat = jnp.where(causal, b_col - b_row + i_row, -jnp.inf)
    m_inter = b_col + m_prev
    m_t = jnp.maximum(m_inter, jnp.max(dmat, axis=-1, keepdims=True))
    w = jnp.exp(dmat - m_t)
    qk = _dot(qb, kb, "nt")
    s = qk * w
    sc = jnp.exp(m_inter - m_t)
    cb = c_prev.astype(BF16)
    num_inter = _dot(qb, cb)
    nb = n_prev.astype(BF16).astype(F32)
    qn = jnp.sum(qb.astype(F32) * nb, axis=-1, keepdims=True)
    num = _dot(s.astype(BF16), vb) + sc * num_inter
    den = jnp.sum(s, axis=-1, keepdims=True) + sc * qn
    e_m = jnp.exp(-m_t)
    nrm = jnp.maximum(jnp.abs(den), e_m)
    hh = num / nrm
    b_last = b_row[:, L - 1:L]
    g_row = b_last - b_row + i_row
    g_col = b_last - b_col + i_col
    m_new = jnp.maximum(b_last + m_prev, jnp.max(g_row, axis=-1, keepdims=True))
    wk = jnp.exp(g_col - m_new)
    decay = jnp.exp(b_last + m_prev - m_new)
    kw = kf * wk
    return dict(qb=qb, kb=kb, vb=vb, w=w, s=s, sc=sc, cb=cb, nb=nb, qn=qn, num_inter=num_inter,
                den=den, e_m=e_m, nrm=nrm, hh=hh, m_new=m_new, wk=wk, decay=decay, kw=kw)


def _mlstm_specs(T, D):
    qk = D // 2
    row = lambda w, j: pl.BlockSpec((CHUNK, w), lambda c, w=w, j=j: (c, j))
    return [row(qk, 0), row(qk, 1), row(D, 1), row(D, 2)]


def _mlstm_fwd(proj, gcol, grow, head_norm):
    T = proj.shape[0]
    D = head_norm.shape[1]
    nc = T // CHUNK
    dk, dv = D // 2 // HEADS, D // HEADS
    kscale = dk ** -0.5
    L = CHUNK

    def body(q_ref, k_ref, v_ref, o_ref, gc_ref, gr_ref, hn_ref, hg_ref, cs_ref, ns_ref, ms_ref, c_s, n_s, m_s):
        @pl.when(pl.program_id(0) == 0)
        def _():
            c_s[...] = jnp.zeros_like(c_s)
            n_s[...] = jnp.zeros_like(n_s)
            m_s[...] = jnp.zeros_like(m_s)

        cs_ref[0] = c_s[...]
        ns_ref[0] = n_s[...]
        ms_ref[0] = m_s[...]
        causal = lax.broadcasted_iota(jnp.int32, (L, L), 1) <= lax.broadcasted_iota(jnp.int32, (L, L), 0)
        gc = gc_ref[...]
        gr = gr_ref[0]
        for h in range(HEADS):
            qf = q_ref[:, h * dk:(h + 1) * dk]
            kf = k_ref[:, h * dk:(h + 1) * dk] * kscale
            vf = v_ref[:, h * dv:(h + 1) * dv]
            m_prev = m_s[h:h + 1, 0:1]
            f = _mlstm_chunk_common(qf, kf, vf, gc, gr, h, c_s[h], n_s[h:h + 1, :], m_prev, causal)
            c_s[h] = f["decay"] * c_s[h] + _dot(f["kw"].astype(BF16), f["vb"], "tn")
            n_s[h:h + 1, :] = f["decay"] * n_s[h:h + 1, :] + jnp.sum(f["kw"], axis=0, keepdims=True)
            m_s[h:h + 1, :] = jnp.broadcast_to(f["m_new"], (1, LANES))
            hh = f["hh"]
            rs = lax.rsqrt(jnp.mean(hh * hh, axis=-1, keepdims=True) + EPS)
            sl = slice(h * dv, (h + 1) * dv)
            hg_ref[:, sl] = (hh * rs * hn_ref[:, sl] * _sigmoid(o_ref[:, sl])).astype(BF16)

    return pl.pallas_call(
        body, name="mlstm_fwd", grid=(nc,),
        in_specs=_mlstm_specs(T, D) + [
            pl.BlockSpec((L, IF_PAD), lambda c: (c, 0)),
            pl.BlockSpec((1, 3 * HEADS, L), lambda c: (c, 0, 0)),
            pl.BlockSpec((1, D), lambda c: (0, 0)),
        ],
        out_specs=[
            pl.BlockSpec((L, D), lambda c: (c, 0)),
            pl.BlockSpec((1, HEADS, dk, dv), lambda c: (c, 0, 0, 0)),
            pl.BlockSpec((1, HEADS, dk), lambda c: (c, 0, 0)),
            pl.BlockSpec((1, HEADS, LANES), lambda c: (c, 0, 0)),
        ],
        out_shape=[
            jax.ShapeDtypeStruct((T, D), BF16),
            jax.ShapeDtypeStruct((nc, HEADS, dk, dv), F32),
            jax.ShapeDtypeStruct((nc, HEADS, dk), F32),
            jax.ShapeDtypeStruct((nc, HEADS, LANES), F32),
        ],
        scratch_shapes=[pltpu.VMEM((HEADS, dk, dv), F32), pltpu.VMEM((HEADS, dk), F32),
                        pltpu.VMEM((HEADS, LANES), F32)],
        compiler_params=_cparams(("arbitrary",)),
    )(proj, proj, proj, proj, gcol, grow, head_norm)


def _mlstm_bwd(proj, gcol, grow, head_norm, dhg, cs, ns, ms):
    T = proj.shape[0]
    D = head_norm.shape[1]
    nc = T // CHUNK
    dk, dv = D // 2 // HEADS, D // HEADS
    qkw = D // 2
    kscale = dk ** -0.5
    L = CHUNK
    pw = 3 * D + IF_PAD

    def body(q_ref, k_ref, v_ref, o_ref, gc_ref, gr_ref, hn_ref, dhg_ref, cs_ref, ns_ref, ms_ref,
             dp_ref, dgain_ref, dbif_ref, dc_s, dn_s):
        @pl.when(pl.program_id(0) == 0)
        def _():
            dc_s[...] = jnp.zeros_like(dc_s)
            dn_s[...] = jnp.zeros_like(dn_s)
            dgain_ref[...] = jnp.zeros_like(dgain_ref)
            dbif_ref[...] = jnp.zeros_like(dbif_ref)

        rowl = lax.broadcasted_iota(jnp.int32, (L, L), 0)
        coll = lax.broadcasted_iota(jnp.int32, (L, L), 1)
        causal = coll <= rowl
        eye = coll == rowl
        lane = lax.broadcasted_iota(jnp.int32, (L, IF_PAD), 1)
        rowg = lax.broadcasted_iota(jnp.int32, (L, IF_PAD), 0)
        gc = gc_ref[...]
        gr = gr_ref[0]
        dgate = jnp.zeros((L, IF_PAD), F32)
        for h in range(HEADS):
            qf = q_ref[:, h * dk:(h + 1) * dk]
            kf = k_ref[:, h * dk:(h + 1) * dk] * kscale
            vf = v_ref[:, h * dv:(h + 1) * dv]
            c_prev = cs_ref[0, h]
            n_prev = ns_ref[0, h:h + 1, :]
            m_prev = ms_ref[0, h:h + 1, 0:1]
            f = _mlstm_chunk_common(qf, kf, vf, gc, gr, h, c_prev, n_prev, m_prev, causal)
            qb, kb, vb, w, s, sc = f["qb"], f["kb"], f["vb"], f["w"], f["s"], f["sc"]
            hh, nrm, den = f["hh"], f["nrm"], f["den"]
            sl = slice(h * dv, (h + 1) * dv)
            dhg_v = dhg_ref[:, sl]
            sig = _sigmoid(o_ref[:, sl])
            gain = hn_ref[:, sl]
            rs = lax.rsqrt(jnp.mean(hh * hh, axis=-1, keepdims=True) + EPS)
            hn_v = hh * rs
            d_o = dhg_v * hn_v * gain * sig * (1.0 - sig)
            dgain_ref[:, sl] += jnp.sum(dhg_v * hn_v * sig, axis=0, keepdims=True)
            dhn = dhg_v * gain * sig
            dh = rs * (dhn - hn_v * jnp.mean(dhn * hn_v, axis=-1, keepdims=True))
            dnum = dh / nrm
            ddn = -jnp.sum(dh * hh, axis=-1, keepdims=True) / nrm
            dden = jnp.where(jnp.abs(den) > f["e_m"], jnp.where(den > 0.0, ddn, -ddn), 0.0)
            dnum_b = dnum.astype(BF16)
            d_s = _dot(dnum_b, vb, "nt") + dden
            dqk = (d_s * w).astype(BF16)
            p = d_s * s
            dcn = dc_s[h]
            dcb = dcn.astype(BF16)
            dnn = dn_s[h:h + 1, :]
            kwb = f["kw"].astype(BF16)
            d_v = _dot(s.astype(BF16), dnum_b, "tn") + _dot(kwb, dcb)
            dkw = _dot(vb, dcb, "nt") + dnn
            d_q = _dot(dqk, kb) + sc * (_dot(dnum_b, f["cb"], "nt") + dden * f["nb"])
            d_kf = _dot(dqk, qb, "tn") + f["wk"] * dkw
            u_col = f["wk"] * jnp.sum(kf * dkw, axis=-1, keepdims=True)
            z = f["decay"] * (jnp.sum(jnp.sum(c_prev * dcn, axis=-1, keepdims=True), axis=0, keepdims=True)
                              + jnp.sum(n_prev * dnn, axis=-1, keepdims=True))
            r_col = sc * (jnp.sum(dnum * f["num_inter"], axis=-1, keepdims=True) + dden * f["qn"])
            rowsum_p = jnp.sum(p, axis=-1, keepdims=True)
            colsum_p = jnp.sum(p, axis=0, keepdims=True)
            colsum_col = jnp.sum(jnp.where(eye, colsum_p, 0.0), axis=-1, keepdims=True)
            di_col = colsum_col + u_col
            db_col = rowsum_p + r_col - colsum_col - u_col
            db_last = jnp.sum(u_col, axis=0, keepdims=True) + z
            qs = (qb.astype(F32) * sc)
            dc_s[h] = f["decay"] * dcn + _dot(qs.astype(BF16), dnum_b, "tn")
            dn_s[h:h + 1, :] = f["decay"] * dnn + jnp.sum(qs * dden, axis=0, keepdims=True)
            dgate = jnp.where(lane == h, di_col, dgate)
            dgate = jnp.where(lane == 2 * HEADS + h,
                              db_col + jnp.where(rowg == L - 1, db_last, 0.0), dgate)
            dp_ref[:, h * dk:(h + 1) * dk] = d_q.astype(BF16)
            dp_ref[:, qkw + h * dk:qkw + (h + 1) * dk] = (d_kf * kscale).astype(BF16)
            dp_ref[:, D + h * dv:D + (h + 1) * dv] = d_v.astype(BF16)
            dp_ref[:, 2 * D + h * dv:2 * D + (h + 1) * dv] = d_o.astype(BF16)
        rc = jnp.where(lane >= 2 * HEADS, dgate, 0.0)
        d = 1
        while d < L:
            rc = rc + jnp.where(rowg < L - d, pltpu.roll(rc, L - d, 0), 0.0)
            d *= 2
        dlf = pltpu.roll(rc, IF_PAD - HEADS, 1)
        dif = jnp.where(lane < HEADS, dgate, 0.0) + jnp.where(
            (lane >= HEADS) & (lane < 2 * HEADS), dlf * (1.0 - jnp.exp(gc)), 0.0)
        dp_ref[:, 3 * D:3 * D + IF_PAD] = dif.astype(BF16)
        dbif_ref[...] += jnp.sum(dif, axis=0, keepdims=True)

    rev = lambda c: nc - 1 - c
    specs = [pl.BlockSpec(s.block_shape, (lambda c, f=s.index_map: f(rev(c)))) for s in _mlstm_specs(T, D)]
    return pl.pallas_call(
        body, name="mlstm_bwd", grid=(nc,),
        in_specs=specs + [
            pl.BlockSpec((L, IF_PAD), lambda c: (rev(c), 0)),
            pl.BlockSpec((1, 3 * HEADS, L), lambda c: (rev(c), 0, 0)),
            pl.BlockSpec((1, D), lambda c: (0, 0)),
            pl.BlockSpec((L, D), lambda c: (rev(c), 0)),
            pl.BlockSpec((1, HEADS, dk, dv), lambda c: (rev(c), 0, 0, 0)),
            pl.BlockSpec((1, HEADS, dk), lambda c: (rev(c), 0, 0)),
            pl.BlockSpec((1, HEADS, LANES), lambda c: (rev(c), 0, 0)),
        ],
        out_specs=[
            pl.BlockSpec((L, pw), lambda c: (rev(c), 0)),
            pl.BlockSpec((1, D), lambda c: (0, 0)),
            pl.BlockSpec((1, IF_PAD), lambda c: (0, 0)),
        ],
        out_shape=[
            jax.ShapeDtypeStruct((T, pw), BF16),
            jax.ShapeDtypeStruct((1, D), F32),
            jax.ShapeDtypeStruct((1, IF_PAD), F32),
        ],
        scratch_shapes=[pltpu.VMEM((HEADS, dk, dv), F32), pltpu.VMEM((HEADS, dk), F32)],
        compiler_params=_cparams(("arbitrary",)),
    )(proj, proj, proj, proj, gcol, grow, head_norm, dhg, cs, ns, ms)


_GELU_C = math.sqrt(2.0 / math.pi)


def _log_sigmoid(x):
    return jnp.minimum(x, 0.0) - jnp.log(1.0 + jnp.exp(-jnp.abs(x)))


def _rglru_recompute(rec_ref, halo_ref, cw_ref, cb_ref, gw_ref, gb_ref, ap_ref, first, W, Lt):
    x = rec_ref[...]
    halo = jnp.where(first, 0.0, halo_ref[...])
    xe = jnp.concatenate([halo, x], axis=0)
    xs = [pltpu.roll(xe, d, 0)[8:8 + Lt] for d in (3, 2, 1)] + [x]
    cw = cw_ref[...]
    rec_c = cb_ref[...] + cw[0:1] * xs[0] + cw[1:2] * xs[1] + cw[2:3] * xs[2] + cw[3:4] * xs[3]
    gates = _dot(rec_c.astype(BF16), gw_ref[0]) + gb_ref[0]
    r = _sigmoid(gates[:, :W])
    ig = _sigmoid(gates[:, W:])
    lsa = _log_sigmoid(ap_ref[...])
    log_a = R_C * r * lsa
    a = jnp.exp(log_a)
    mult = jnp.sqrt(1.0 - jnp.exp(2.0 * log_a))
    return dict(xs=xs, cw=cw, rec_c=rec_c, r=r, ig=ig, lsa=lsa, a=a, mult=mult)


def _rglru_specs(T, D, Lt, tmap):
    W = D // RBLOCKS
    hb = Lt // 8
    return [
        pl.BlockSpec((Lt, W), lambda g, i: (tmap(i), g)),
        pl.BlockSpec((Lt, W), lambda g, i: (tmap(i), RBLOCKS + g)),
        pl.BlockSpec((8, W), lambda g, i: (jnp.maximum(tmap(i) * hb - 1, 0), RBLOCKS + g)),
        pl.BlockSpec((CONV_W, W), lambda g, i: (0, g)),
        pl.BlockSpec((1, W), lambda g, i: (0, g)),
        pl.BlockSpec((1, W, 2 * W), lambda g, i: (g, 0, 0)),
        pl.BlockSpec((1, 1, 2 * W), lambda g, i: (g, 0, 0)),
        pl.BlockSpec((1, W), lambda g, i: (0, g)),
    ]


def _rglru_fwd(pr, conv_w, conv_b, gate_w, gate_b, a_param):
    T = pr.shape[0]
    D = pr.shape[1] // 2
    W = D // RBLOCKS
    Lt = _pick(T, 256, 8)
    nt = T // Lt

    def body(gate_ref, rec_ref, halo_ref, cw_ref, cb_ref, gw_ref, gb_ref, ap_ref, y_ref, hs_ref, carry):
        i = pl.program_id(1)

        @pl.when(i == 0)
        def _():
            carry[...] = jnp.zeros_like(carry)

        f = _rglru_recompute(rec_ref, halo_ref, cw_ref, cb_ref, gw_ref, gb_ref, ap_ref, i == 0, W, Lt)
        row = lax.broadcasted_iota(jnp.int32, (Lt, W), 0)
        a_c = f["a"]
        u_c = f["mult"] * (f["ig"] * f["rec_c"])
        d = 1
        while d < Lt:
            msk = row >= d
            u_c = jnp.where(msk, a_c * pltpu.roll(u_c, d, 0) + u_c, u_c)
            a_c = jnp.where(msk, a_c * pltpu.roll(a_c, d, 0), a_c)
            d *= 2
        h = u_c + a_c * carry[0:1, :]
        carry[0:1, :] = h[Lt - 1:Lt, :]
        hs_ref[...] = h
        gb = gate_ref[...]
        t = jnp.tanh(_GELU_C * (gb + 0.044715 * gb * gb * gb))
        y_ref[...] = (0.5 * gb * (1.0 + t) * h).astype(BF16)

    blk = pl.BlockSpec((Lt, W), lambda g, i: (i, g))
    return pl.pallas_call(
        body, name="rglru_fwd", grid=(RBLOCKS, nt),
        in_specs=_rglru_specs(T, D, Lt, lambda i: i),
        out_specs=[blk, blk],
        out_shape=[jax.ShapeDtypeStruct((T, D), BF16), jax.ShapeDtypeStruct((T, D), F32)],
        scratch_shapes=[pltpu.VMEM((8, W), F32)],
        compiler_params=_cparams(("parallel", "arbitrary")),
    )(pr, pr, pr, conv_w, conv_b, gate_w, gate_b, a_param)


def _rglru_bwd(pr, hs, dy, conv_w, conv_b, gate_w, gate_b, a_param):
    T = pr.shape[0]
    D = pr.shape[1] // 2
    W = D // RBLOCKS
    Lt = _pick(T, 256, 8)
    nt = T // Lt
    hb = Lt // 8
    tmap = lambda i: nt - 1 - i

    def body(gate_ref, rec_ref, halo_ref, cw_ref, cb_ref, gw_ref, gb_ref, ap_ref, hs_ref, hh_ref, dy_ref,
             dgate_ref, drec_ref, dcw_ref, dcb_ref, dgw_ref, dgb_ref, dap_ref, lam_s, drc_s):
        i = pl.program_id(1)
        ti = nt - 1 - i

        @pl.when(i == 0)
        def _():
            lam_s[...] = jnp.zeros_like(lam_s)
            drc_s[...] = jnp.zeros_like(drc_s)
            dcw_ref[...] = jnp.zeros_like(dcw_ref)
            dcb_ref[...] = jnp.zeros_like(dcb_ref)
            dgw_ref[...] = jnp.zeros_like(dgw_ref)
            dgb_ref[...] = jnp.zeros_like(dgb_ref)
            dap_ref[...] = jnp.zeros_like(dap_ref)

        f = _rglru_recompute(rec_ref, halo_ref, cw_ref, cb_ref, gw_ref, gb_ref, ap_ref, ti == 0, W, Lt)
        a, mult, ig, r, rec_c, lsa, xs, cw = (f[k] for k in ("a", "mult", "ig", "r", "rec_c", "lsa", "xs", "cw"))
        row = lax.broadcasted_iota(jnp.int32, (Lt, W), 0)
        gb = gate_ref[...]
        t = jnp.tanh(_GELU_C * (gb + 0.044715 * gb * gb * gb))
        gel = 0.5 * gb * (1.0 + t)
        dgel = 0.5 * (1.0 + t) + 0.5 * gb * (1.0 - t * t) * _GELU_C * (1.0 + 3.0 * 0.044715 * gb * gb)
        h = hs_ref[...]
        h_first = jnp.where(ti == 0, 0.0, hh_ref[7:8, :])
        h_prev = jnp.where(row == 0, h_first, pltpu.roll(h, 1, 0))
        dy_v = dy_ref[...]
        d_gb = dy_v * h * dgel
        c_c = jnp.where(row < Lt - 1, pltpu.roll(a, Lt - 1, 0), 0.0)
        g_c = dy_v * gel + jnp.where(row == Lt - 1, lam_s[0:1, :], 0.0)
        d = 1
        while d < Lt:
            msk = row < Lt - d
            g_c = jnp.where(msk, c_c * pltpu.roll(g_c, Lt - d, 0) + g_c, g_c)
            c_c = jnp.where(msk, c_c * pltpu.roll(c_c, Lt - d, 0), c_c)
            d *= 2
        lam = g_c
        lam_s[0:1, :] = (a * lam)[0:1, :]
        d_mult = lam * ig * rec_c
        d_loga = lam * h_prev * a - d_mult * (a * a) / mult
        d_ig = lam * mult * rec_c
        d_rec = lam * mult * ig
        d_r = d_loga * (R_C * lsa)
        dap_ref[...] += jnp.sum(d_loga * (R_C * r), axis=0, keepdims=True) * (1.0 - jnp.exp(lsa))
        dgates = jnp.concatenate([d_r * r * (1.0 - r), d_ig * ig * (1.0 - ig)], axis=1)
        dgb_ref[0] += jnp.sum(dgates, axis=0, keepdims=True)
        dgates_b = dgates.astype(BF16)
        dgw_ref[0] += _dot(rec_c.astype(BF16), dgates_b, "tn")
        d_rec = d_rec + _dot(dgates_b, gw_ref[0], "nt")
        dcb_ref[...] += jnp.sum(d_rec, axis=0, keepdims=True)
        dcw_ref[...] += jnp.concatenate(
            [jnp.sum(d_rec * xs[j], axis=0, keepdims=True) for j in range(CONV_W)], axis=0)
        ext = jnp.concatenate([d_rec, drc_s[...]], axis=0)
        up = lambda s: pltpu.roll(ext, Lt + 8 - s, 0)[:Lt]
        d_x = cw[3:4] * d_rec + cw[2:3] * up(1) + cw[1:2] * up(2) + cw[0:1] * up(3)
        drc_s[...] = d_rec[0:8, :]
        dgate_ref[...] = d_gb.astype(BF16)
        drec_ref[...] = d_x.astype(BF16)

    blk = pl.BlockSpec((Lt, W), lambda g, i: (tmap(i), g))
    return pl.pallas_call(
        body, name="rglru_bwd", grid=(RBLOCKS, nt),
        in_specs=_rglru_specs(T, D, Lt, tmap) + [
            blk,
            pl.BlockSpec((8, W), lambda g, i: (jnp.maximum(tmap(i) * hb - 1, 0), g)),
            blk,
        ],
        out_specs=[
            blk,
            blk,
            pl.BlockSpec((CONV_W, W), lambda g, i: (0, g)),
            pl.BlockSpec((1, W), lambda g, i: (0, g)),
            pl.BlockSpec((1, W, 2 * W), lambda g, i: (g, 0, 0)),
            pl.BlockSpec((1, 1, 2 * W), lambda g, i: (g, 0, 0)),
            pl.BlockSpec((1, W), lambda g, i: (0, g)),
        ],
        out_shape=[
            jax.ShapeDtypeStruct((T, D), BF16),
            jax.ShapeDtypeStruct((T, D), BF16),
            jax.ShapeDtypeStruct((CONV_W, D), F32),
            jax.ShapeDtypeStruct((1, D), F32),
            jax.ShapeDtypeStruct((RBLOCKS, W, 2 * W), F32),
            jax.ShapeDtypeStruct((RBLOCKS, 1, 2 * W), F32),
            jax.ShapeDtypeStruct((1, D), F32),
        ],
        scratch_shapes=[pltpu.VMEM((8, W), F32), pltpu.VMEM((8, W), F32)],
        compiler_params=_cparams(("parallel", "arbitrary")),
    )(pr, pr, pr, conv_w, conv_b, gate_w, gate_b, a_param, hs, hs, dy)


def _adamw(name, w, gs, m, v):
    L, R, C = w.shape
    tr = _pick(R, max(8, (1 << 19) // C), 8)

    def body(*refs):
        w_ref, m_ref, v_ref = refs[:3]
        g_refs = refs[3:3 + L]
        go_ref, d_ref, nm_ref, nv_ref = refs[3 + L:]
        l = pl.program_id(0)
        gv = g_refs[0][...]
        for q in range(1, L):
            gv = jnp.where(l == q, g_refs[q][...], gv)
        mv = ADAM_B1 * m_ref[0] + (1.0 - ADAM_B1) * gv
        vv = ADAM_B2 * v_ref[0] + (1.0 - ADAM_B2) * (gv * gv)
        m_hat = mv / (1.0 - ADAM_B1 ** ADAM_STEP)
        v_hat = vv / (1.0 - ADAM_B2 ** ADAM_STEP)
        go_ref[0] = gv
        d_ref[0] = -ADAM_LR * (m_hat / (jnp.sqrt(v_hat) + ADAM_EPS) + ADAM_WD * w_ref[0])
        nm_ref[0] = mv
        nv_ref[0] = vv

    blk = pl.BlockSpec((1, tr, C), lambda l, i: (l, i, 0))
    g_specs = [pl.BlockSpec((tr, C), lambda l, i, q=q: (jnp.where(l == q, i, 0), 0)) for q in range(L)]
    sds = jax.ShapeDtypeStruct((L, R, C), F32)
    return pl.pallas_call(
        body, name=name, grid=(L, R // tr),
        in_specs=[blk] * 3 + g_specs, out_specs=[blk] * 4, out_shape=[sds] * 4,
        compiler_params=_cparams(("parallel", "parallel")),
    )(w, m, v, *gs)


def _step(x, tgt, p, weights, gather_plan, scatter_plan):
    T, D = x.shape
    pw = 3 * D + IF_PAD
    nc = T // CHUNK
    tm = _pick(T, 1024, 128)
    tkt = _pick(T, 1024, 128)
    td = _pick(D, 1024, 128)
    vec = lambda a, l: a[l:l + 1]
    w = dict(weights)
    g, landed = {}, {}

    def mm(name, a_op, b_op, mode, M, N, K, out_defs, **kw):
        carry = None
        if name in gather_plan:
            carry = gather_plan[name]
        elif name in scatter_plan:
            carry = scatter_plan[name](g)
        outs, extra = _mm(name, a_op, b_op, mode, M, N, K, out_defs, carry=carry, **kw)
        if name in gather_plan:
            for n, a in zip(carry.names, extra):
                w[n] = carry.post[n](a) if n in carry.post else a
        elif carry is not None:
            landed.update(zip(carry.names, extra))
        return outs

    def ffn_fwd(l, h_in):
        dff = w[f"ffn{l}_w_out"].shape[0]
        hn, r = _rms_fwd(f"ffn{l}_norm", h_in, vec(p["norm_ffn"], l))
        tn = _pick(2 * dff, 1024, 128)
        (gu,) = mm(f"ffn{l}_in", _a_plain(hn, "nn", tm, D), _b_plain(w[f"ffn{l}_w_in"], "nn", tn, D), "nn",
                   T, 2 * dff, D, [(BF16, 2 * dff)], tm=tm, tn=tn, tk=D)
        tk = _pick(dff, 1408, 128)
        (h_out,) = mm(f"ffn{l}_out", _a_swiglu(gu, "nn", tm, tk), _b_plain(w[f"ffn{l}_w_out"], "nn", td, tk), "nn",
                      T, D, dff, [(F32, D)], tm=tm, tn=td, tk=tk, epi=_epi_resid, extras=[(h_in, td, 0)])
        return h_out, (hn, r, gu)

    def ffn_bwd(l, h_in, saved, dh, dhb):
        hn, r, gu = saved
        dff = w[f"ffn{l}_w_out"].shape[0]
        tn = _pick(dff, 512, 128)
        dg, du = mm(f"ffn{l}_dact", _a_plain(dhb, "nt", tm, D), _b_plain(w[f"ffn{l}_w_out"], "nt", tn, D), "nt",
                    T, dff, D, [(BF16, dff), (BF16, dff)], tm=tm, tn=tn, tk=D, epi=_epi_swiglu_bwd,
                    extras=[(gu, tn, 0), (gu, tn, dff // tn)])
        tmf = _pick(dff, 512, 128)
        (g[f"ffn{l}_w_out"],) = mm(f"ffn{l}_dwout", _a_swiglu(gu, "tn", tmf, tkt), _b_plain(dhb, "tn", td, tkt), "tn",
                                   dff, D, T, [(BF16, D)], tm=tmf, tn=td, tk=tkt)
        tn2 = _pick(dff, 1024, 128)
        (g[f"ffn{l}_w_in"],) = mm(f"ffn{l}_dwin", _a_plain(hn, "tn", td, tkt), _b_split_n(dg, du, tn2, tkt), "tn",
                                  D, 2 * dff, T, [(BF16, 2 * dff)], tm=td, tn=tn2, tk=tkt)
        tk2 = _pick(dff, 2048, 128)
        (dhn,) = mm(f"ffn{l}_dx", _a_split_k(dg, du, tm, tk2), _b_plain(w[f"ffn{l}_w_in"], "nt", td, tk2), "nt",
                    T, D, 2 * dff, [(F32, D)], tm=tm, tn=td, tk=tk2)
        dh, dhb, dgn = _rms_bwd(f"ffn{l}_dnorm", h_in, r, dhn, vec(p["norm_ffn"], l), dh)
        return dh, dhb, dgn

    h0 = x
    hn0, r0 = _rms_fwd("mix0_norm", h0, vec(p["norm_mix"], 0))
    tnp = _pick(pw, 1024, 128)
    (proj,) = mm("m_proj", _a_plain(hn0, "nn", tm, D), _b_plain(w["m_w_in"], "nn", tnp, D), "nn",
                 T, pw, D, [(F32, pw)], tm=tm, tn=tnp, tk=D)
    gcol = _mlstm_gates(proj, p["m_b_if"], D)
    grow = gcol[:, :3 * HEADS].reshape(nc, CHUNK, 3 * HEADS).transpose(0, 2, 1)
    hg, cs, ns, ms = _mlstm_fwd(proj, gcol, grow, p["m_head_norm"])
    (h1,) = mm("m_out", _a_plain(hg, "nn", tm, D), _b_plain(w["m_w_out"], "nn", td, D), "nn",
               T, D, D, [(F32, D)], tm=tm, tn=td, tk=D, epi=_epi_resid, extras=[(h0, td, 0)])
    h2, ffn0 = ffn_fwd(0, h1)
    hn2, r2 = _rms_fwd("mix1_norm", h2, vec(p["norm_mix"], 1))
    (pr,) = mm("r_proj", _a_plain(hn2, "nn", tm, D), _b_plain(w["r_w_in"], "nn", td, D), "nn",
               T, 2 * D, D, [(F32, 2 * D)], tm=tm, tn=td, tk=D)
    y, hs = _rglru_fwd(pr, p["conv_w"], p["conv_b"], w["gate_w"], p["gate_b"], p["a_param"])
    (h3,) = mm("r_out", _a_plain(y, "nn", tm, D), _b_plain(w["r_w_out"], "nn", td, D), "nn",
               T, D, D, [(F32, D)], tm=tm, tn=td, tk=D, epi=_epi_resid, extras=[(h2, td, 0)])
    h4, ffn1 = ffn_fwd(1, h3)
    loss, dh, dhb, g["norm_final"] = _loss_head(h4, p["norm_final"], tgt)

    dh, dhb, dnf1 = ffn_bwd(1, h3, ffn1, dh, dhb)
    (dy,) = mm("r_dy", _a_plain(dhb, "nt", tm, D), _b_plain(w["r_w_out"], "nt", td, D), "nt",
               T, D, D, [(F32, D)], tm=tm, tn=td, tk=D)
    (g["r_w_out"],) = mm("r_dwout", _a_plain(y, "tn", td, tkt), _b_plain(dhb, "tn", td, tkt), "tn",
                         D, D, T, [(BF16, D)], tm=td, tn=td, tk=tkt)
    dgate, drec, g["conv_w"], g["conv_b"], dgw, g["gate_b"], g["a_param"] = _rglru_bwd(
        pr, hs, dy, p["conv_w"], p["conv_b"], w["gate_w"], p["gate_b"], p["a_param"])
    g["gate_w"] = dgw.astype(BF16)
    (g["r_w_in"],) = mm("r_dwin", _a_plain(hn2, "tn", td, tkt), _b_split_n(dgate, drec, td, tkt), "tn",
                        D, 2 * D, T, [(BF16, 2 * D)], tm=td, tn=td, tk=tkt)
    (dhn2,) = mm("r_dx", _a_split_k(dgate, drec, tm, td), _b_plain(w["r_w_in"], "nt", td, td), "nt",
                 T, D, 2 * D, [(F32, D)], tm=tm, tn=td, tk=td)
    dh, dhb, dnm1 = _rms_bwd("mix1_dnorm", h2, r2, dhn2, vec(p["norm_mix"], 1), dh)
    dh, dhb, dnf0 = ffn_bwd(0, h1, ffn0, dh, dhb)
    (dhg,) = mm("m_dhg", _a_plain(dhb, "nt", tm, D), _b_plain(w["m_w_out"], "nt", td, D), "nt",
                T, D, D, [(F32, D)], tm=tm, tn=td, tk=D)
    (g["m_w_out"],) = mm("m_dwout", _a_plain(hg, "tn", td, tkt), _b_plain(dhb, "tn", td, tkt), "tn",
                         D, D, T, [(BF16, D)], tm=td, tn=td, tk=tkt)
    dproj, g["m_head_norm"], g["m_b_if"] = _mlstm_bwd(proj, gcol, grow, p["m_head_norm"], dhg, cs, ns, ms)
    (g["m_w_in"],) = mm("m_dwin", _a_plain(hn0, "tn", td, tkt), _b_plain(dproj, "tn", tnp, tkt), "tn",
                        D, pw, T, [(BF16, pw)], tm=td, tn=tnp, tk=tkt)
    (dhn0,) = mm("m_dx", _a_plain(dproj, "nt", tm, tnp), _b_plain(w["m_w_in"], "nt", td, tnp), "nt",
                 T, D, pw, [(F32, D)], tm=tm, tn=td, tk=tnp)
    grad_x, _, dnm0 = _rms_bwd("mix0_dnorm", h0, r0, dhn0, vec(p["norm_mix"], 0), dh)
    g["norm_mix"] = jnp.concatenate([dnm0, dnm1], axis=0)
    g["norm_ffn"] = jnp.concatenate([dnf0, dnf1], axis=0)
    return loss, grad_x, g, landed


_MESH = pl.DeviceIdType.MESH
_HBM = pl.BlockSpec(memory_space=pltpu.HBM)
N_CHIPS = 4


def _place():
    x, y, c = lax.axis_index("x"), lax.axis_index("y"), lax.axis_index("c")
    chips = [(1 - x, y), (x, 1 - y), (1 - x, 1 - y)]
    return x, y, c, 2 * x + y, chips, [2 * px + py for px, py in chips]


class _Piece:
    def __init__(self, kind, rows, cols):
        self.kind, self.rows, self.cols, self.hr = kind, rows, cols, rows // 2

    def full_shape(self):
        return {"col": (self.rows, N_CHIPS * self.cols), "row": (N_CHIPS * self.rows, self.cols),
                "stack": (N_CHIPS, self.rows, self.cols)}[self.kind]

    def region(self, ref, s, h):
        rows = pl.ds(h * self.hr, self.hr)
        if self.kind == "col":
            return ref.at[rows, pl.ds(pl.multiple_of(s * self.cols, LANES), self.cols)]
        if self.kind == "row":
            return ref.at[pl.ds(pl.multiple_of(s * self.rows + h * self.hr, 16), self.hr), :]
        return ref.at[s, rows, :]

    def shard_half(self, ref, h):
        return ref.at[pl.ds(h * self.hr, self.hr), :]


class _Gather:
    def __init__(self, names, shards, pieces, post=None):
        n = len(names)
        self.names, self.ins, self.pieces, self.post = list(names), list(shards), list(pieces), post or {}
        self.out_shapes = [jax.ShapeDtypeStruct(p.full_shape(), s.dtype) for p, s in zip(pieces, shards)]
        self.scratch = [pltpu.SemaphoreType.DMA((n, 2))] + [pltpu.SemaphoreType.DMA((n, 3))] * 4

    def _copies(self, ins, outs, sems):
        loc, snd, rcv, fsnd, frcv = sems
        x, y, c, j, chips, cj = _place()

        def local(w, h):
            return pltpu.make_async_copy(self.pieces[w].shard_half(ins[w], h), self.pieces[w].region(outs[w], j, h),
                                         loc.at[w, h])

        def ici_send(w, k):
            return pltpu.make_async_remote_copy(
                src_ref=self.pieces[w].shard_half(ins[w], c), dst_ref=self.pieces[w].region(outs[w], j, c),
                send_sem=snd.at[w, k], recv_sem=rcv.at[w, k], device_id=(*chips[k], c), device_id_type=_MESH)

        def ici_recv(w, k):
            region = self.pieces[w].region(outs[w], cj[k], c)
            return pltpu.make_async_remote_copy(
                src_ref=region, dst_ref=region, send_sem=snd.at[w, k], recv_sem=rcv.at[w, k],
                device_id=(*chips[k], c), device_id_type=_MESH)

        def d2d(w, k, h):
            region = self.pieces[w].region(outs[w], cj[k], h)
            return pltpu.make_async_remote_copy(
                src_ref=region, dst_ref=region, send_sem=fsnd.at[w, k], recv_sem=frcv.at[w, k],
                device_id=(x, y, 1 - c), device_id_type=_MESH)

        return c, local, ici_send, ici_recv, d2d

    def start(self, ins, outs, sems):
        _, local, ici_send, _, _ = self._copies(ins, outs, sems)
        for w in range(len(self.names)):
            for k in range(3):
                ici_send(w, k).start()
        for w in range(len(self.names)):
            for h in range(2):
                local(w, h).start()

    def finish(self, ins, outs, sems):
        c, local, ici_send, ici_recv, d2d = self._copies(ins, outs, sems)
        n = len(self.names)
        for w in range(n):
            for k in range(3):
                ici_recv(w, k).wait_recv()
                d2d(w, k, c).start()
        for w in range(n):
            for k in range(3):
                d2d(w, k, 1 - c).wait_recv()
        for w in range(n):
            for k in range(3):
                ici_send(w, k).wait_send()
                d2d(w, k, c).wait_send()
            for h in range(2):
                local(w, h).wait()


class _Scatter:
    def __init__(self, names, grads, pieces):
        n = len(names)
        self.names, self.ins, self.pieces = list(names), list(grads), list(pieces)
        self.out_shapes = [jax.ShapeDtypeStruct((7, p.hr, p.cols), g.dtype) for p, g in zip(pieces, grads)]
        self.scratch = [pltpu.SemaphoreType.DMA((n, 7))] * 2

    def _copy(self, ins, outs, sems, w, k):
        x, y, c = lax.axis_index("x"), lax.axis_index("y"), lax.axis_index("c")
        px, py, pc = ((1 - x) if k & 4 else x, (1 - y) if k & 2 else y, (1 - c) if k & 1 else c)
        return pltpu.make_async_remote_copy(
            src_ref=self.pieces[w].region(ins[w], 2 * px + py, pc), dst_ref=outs[w].at[k - 1],
            send_sem=sems[0].at[w, k - 1], recv_sem=sems[1].at[w, k - 1], device_id=(px, py, pc),
            device_id_type=_MESH)

    def start(self, ins, outs, sems):
        for w in range(len(self.names)):
            for k in range(1, 8):
                self._copy(ins, outs, sems, w, k).start()

    def finish(self, ins, outs, sems):
        for w in range(len(self.names)):
            for k in range(1, 8):
                self._copy(ins, outs, sems, w, k).wait()


def _run_exchange(name, ex):
    ni, no = len(ex.ins), len(ex.out_shapes)

    def body(*refs):
        ins, outs, sems = refs[:ni], refs[ni:ni + no], refs[ni + no:]
        ex.start(ins, outs, sems)
        ex.finish(ins, outs, sems)

    return pl.pallas_call(
        body, name=name, in_specs=[_HBM] * ni, out_specs=[_HBM] * no, out_shape=list(ex.out_shapes),
        scratch_shapes=list(ex.scratch),
    )(*ex.ins)


def _rs_sum(name, piece, grad, landed, jc_idx):
    hr, C = piece.hr, piece.cols
    br = _pick(hr, max(16, (1 << 18) // C), 16)
    nb = hr // br
    if piece.kind == "col":
        g_spec = pl.BlockSpec((br, C), lambda i, jc: (jc[1] * nb + i, jc[0]))
    elif piece.kind == "row":
        g_spec = pl.BlockSpec((br, C), lambda i, jc: ((jc[0] * 2 + jc[1]) * nb + i, 0))
    else:
        g_spec = pl.BlockSpec((None, br, C), lambda i, jc: (jc[0], jc[1] * nb + i, 0))

    def body(jc_ref, g_ref, l_ref, o_ref):
        acc = g_ref[...].astype(F32)
        for k in range(7):
            acc = acc + l_ref[k].astype(F32)
        o_ref[0] = acc

    return pl.pallas_call(
        body, name=name,
        grid_spec=pltpu.PrefetchScalarGridSpec(
            num_scalar_prefetch=1, grid=(nb,),
            in_specs=[g_spec, pl.BlockSpec((7, br, C), lambda i, jc: (0, i, 0))],
            out_specs=pl.BlockSpec((1, br, C), lambda i, jc: (jc[1], i, 0))),
        out_shape=jax.ShapeDtypeStruct((2, hr, C), F32),
        compiler_params=_cparams(("parallel",)),
    )(jc_idx, grad, landed)


def _rs_share_halves(rs):
    n = len(rs)

    def body(*refs):
        outs = refs[n:2 * n]
        send_sems, recv_sems = refs[2 * n:]
        x, y, c, _, _, _ = _place()
        cps = []
        for w in range(n):
            cps.append(pltpu.make_async_remote_copy(
                src_ref=outs[w].at[c], dst_ref=outs[w].at[c], send_sem=send_sems.at[w],
                recv_sem=recv_sems.at[w], device_id=(x, y, 1 - c), device_id_type=_MESH))
            cps[-1].start()
        for w in range(n):
            cps[w].wait_send()
            pltpu.make_async_remote_copy(
                src_ref=outs[w].at[1 - c], dst_ref=outs[w].at[1 - c], send_sem=send_sems.at[w],
                recv_sem=recv_sems.at[w], device_id=(x, y, 1 - c), device_id_type=_MESH).wait_recv()

    return pl.pallas_call(
        body, name="rs_share_halves",
        in_specs=[_HBM] * n, out_specs=[_HBM] * n,
        out_shape=[jax.ShapeDtypeStruct(a.shape, a.dtype) for a in rs],
        input_output_aliases={w: w for w in range(n)},
        scratch_shapes=[pltpu.SemaphoreType.DMA((n,))] * 2,
    )(*rs)


def _allreduce_small(v):
    R = v.shape[0]

    def body(v_ref, o_ref, slots, send_sems, recv_sems):
        x, y, c = lax.axis_index("x"), lax.axis_index("y"), lax.axis_index("c")
        me = 4 * x + 2 * y + c
        cps = []
        for k in range(1, 8):
            peer = ((1 - x) if k & 4 else x, (1 - y) if k & 2 else y, (1 - c) if k & 1 else c)
            cps.append(pltpu.make_async_remote_copy(
                src_ref=v_ref, dst_ref=slots.at[k - 1], send_sem=send_sems.at[k - 1],
                recv_sem=recv_sems.at[k - 1], device_id=peer, device_id_type=_MESH))
            cps[-1].start()
        for cp in cps:
            cp.wait()
        acc = jnp.zeros((R, LANES), F32)
        for d in range(8):
            k = ((d // 4) ^ x) * 4 + (((d // 2) % 2) ^ y) * 2 + ((d % 2) ^ c)
            other = slots[jnp.maximum(k - 1, 0)]
            acc = acc + jnp.where(d == me, v_ref[...], other)
        o_ref[...] = acc

    vm = pl.BlockSpec(memory_space=pltpu.VMEM)
    return pl.pallas_call(
        body, name="allreduce_small",
        in_specs=[vm], out_specs=vm,
        out_shape=jax.ShapeDtypeStruct((R, LANES), F32),
        scratch_shapes=[pltpu.VMEM((7, R, LANES), F32), pltpu.SemaphoreType.DMA((7,)), pltpu.SemaphoreType.DMA((7,))],
    )(v)


def _rows(a):
    flat = a.reshape(-1)
    return jnp.pad(flat, (0, (-flat.shape[0]) % LANES)).reshape(-1, LANES)


def _pack_rows(parts, mult=8):
    v = jnp.concatenate([_rows(a) for a in parts], axis=0)
    return jnp.pad(v, ((0, (-v.shape[0]) % mult), (0, 0)))


def _unpack_rows(v, shapes):
    out, r = [], 0
    for s in shapes:
        size = math.prod(s)
        nr = -(-size // LANES)
        out.append(v[r:r + nr].reshape(-1)[:size].reshape(s))
        r += nr
    return out


def kernel(x, norm_mix, norm_ffn, norm_final, m_w_in, m_b_if, m_head_norm, m_w_out, r_w_in, r_conv_w, r_conv_b, r_gate_w, r_gate_b, r_a_param, r_w_out, ffn_w_in, ffn_w_out, loss_target, m_norm_mix, m_norm_ffn, m_norm_final, m_m_w_in, m_m_b_if, m_m_head_norm, m_m_w_out, m_r_w_in, m_r_conv_w, m_r_conv_b, m_r_gate_w, m_r_gate_b, m_r_a_param, m_r_w_out, m_ffn_w_in, m_ffn_w_out, v_norm_mix, v_norm_ffn, v_norm_final, v_m_w_in, v_m_b_if, v_m_head_norm, v_m_w_out, v_r_w_in, v_r_conv_w, v_r_conv_b, v_r_gate_w, v_r_gate_b, v_r_a_param, v_r_w_out, v_ffn_w_in, v_ffn_w_out):
    names = ["norm_mix", "norm_ffn", "norm_final", "m_w_in", "m_b_if", "m_head_norm", "m_w_out", "r_w_in", "r_conv_w",
             "r_conv_b", "r_gate_w", "r_gate_b", "r_a_param", "r_w_out", "ffn_w_in", "ffn_w_out"]
    w = dict(zip(names, [norm_mix, norm_ffn, norm_final, m_w_in, m_b_if, m_head_norm, m_w_out, r_w_in, r_conv_w,
                         r_conv_b, r_gate_w, r_gate_b, r_a_param, r_w_out, ffn_w_in, ffn_w_out]))
    mom = dict(zip(names, [m_norm_mix, m_norm_ffn, m_norm_final, m_m_w_in, m_m_b_if, m_m_head_norm, m_m_w_out, m_r_w_in,
                           m_r_conv_w, m_r_conv_b, m_r_gate_w, m_r_gate_b, m_r_a_param, m_r_w_out, m_ffn_w_in, m_ffn_w_out]))
    var = dict(zip(names, [v_norm_mix, v_norm_ffn, v_norm_final, v_m_w_in, v_m_b_if, v_m_head_norm, v_m_w_out, v_r_w_in,
                           v_r_conv_w, v_r_conv_b, v_r_gate_w, v_r_gate_b, v_r_a_param, v_r_w_out, v_ffn_w_in, v_ffn_w_out]))
    D = x.shape[-1]
    W = D // RBLOCKS
    nproj = m_w_in.shape[2] * N_CHIPS
    shard_col = lax.axis_index("x") * 2 + lax.axis_index("y")
    jc_idx = jnp.stack([shard_col, lax.axis_index("c")]).astype(jnp.int32)

    def cols(g, lead):
        nl = len(lead)
        g = g.reshape((N_CHIPS,) + lead + (-1,))
        g = jnp.moveaxis(g, 0, nl)
        return g.reshape(lead + (-1,))

    def uncols(a, lead):
        nl = len(lead)
        a = a.reshape(lead + (N_CHIPS, -1))
        a = jnp.moveaxis(a, nl, 0)
        return a.reshape(N_CHIPS, math.prod(lead), -1)

    shard = {"m_w_in": m_w_in[0], "m_w_out": m_w_out[0], "r_w_in": r_w_in[0],
             "gate_w": r_gate_w[0].reshape(-1, r_gate_w.shape[-1]), "r_w_out": r_w_out[0]}
    kind = {"m_w_in": "stack", "m_w_out": "row", "r_w_in": "col", "gate_w": "stack", "r_w_out": "row"}
    for l in range(2):
        shard[f"ffn{l}_w_in"], kind[f"ffn{l}_w_in"] = ffn_w_in[l], "col"
        shard[f"ffn{l}_w_out"], kind[f"ffn{l}_w_out"] = ffn_w_out[l], "row"
    shard = {n: a.astype(BF16) for n, a in shard.items()}
    piece = {n: _Piece(kind[n], *shard[n].shape) for n in shard}
    small_sharded = ["r_conv_w", "r_conv_b", "r_gate_b", "r_a_param"]
    small_pack = _pack_rows([w[n] for n in small_sharded], mult=16)
    post = {"m_w_in": lambda a: jnp.pad(cols(a, (D,)), ((0, 0), (0, 3 * D + IF_PAD - nproj))),
            "gate_w": lambda a: cols(a, (RBLOCKS, W))}

    def gather(names):
        return _Gather(names, [shard[n] for n in names], [piece[n] for n in names], post)

    first = _Gather(["m_w_in", "m_w_out", "small"], [shard["m_w_in"], shard["m_w_out"], small_pack],
                    [piece["m_w_in"], piece["m_w_out"], _Piece("stack", *small_pack.shape)], post)
    got = dict(zip(first.names, _run_exchange("gather_first", first)))
    weights = {"m_w_in": post["m_w_in"](got["m_w_in"]), "m_w_out": got["m_w_out"]}
    gather_plan = {"m_proj": gather(["ffn0_w_in"]), "m_out": gather(["ffn0_w_out"]),
                   "ffn0_in": gather(["r_w_in", "gate_w", "r_w_out", "ffn1_w_out"]),
                   "ffn0_out": gather(["ffn1_w_in"])}
    sp = got["small"]
    ws = D // N_CHIPS
    wg = 2 * W // N_CHIPS
    r1, r2, r3 = CONV_W * ws // LANES, (CONV_W + 1) * ws // LANES, ((CONV_W + 1) * ws + RBLOCKS * wg) // LANES
    conv_w_f = sp[:, :r1].reshape(N_CHIPS, CONV_W, ws).transpose(1, 0, 2).reshape(CONV_W, D)
    conv_b_f = sp[:, r1:r2].reshape(1, D)
    gate_b_f = sp[:, r2:r3].reshape(N_CHIPS, RBLOCKS, wg).transpose(1, 0, 2).reshape(RBLOCKS, 1, 2 * W)
    a_param_f = sp[:, r3:r3 + ws // LANES].reshape(1, D)
    p = dict(
        norm_mix=norm_mix, norm_ffn=norm_ffn, norm_final=norm_final.reshape(1, D),
        m_b_if=jnp.pad(m_b_if, ((0, 0), (0, IF_PAD - m_b_if.shape[1]))), m_head_norm=m_head_norm,
        conv_w=conv_w_f, conv_b=conv_b_f, gate_b=gate_b_f, a_param=a_param_f,
    )

    gsrc = {}

    def scatter(names):
        def make(g):
            for n in names:
                if n == "m_w_in":
                    gsrc[n] = uncols(g[n][:, :nproj], (D,))
                elif n == "gate_w":
                    gsrc[n] = uncols(g[n], (RBLOCKS, W))
                else:
                    gsrc[n] = g[n]
            return _Scatter(names, [gsrc[n] for n in names], [piece[n] for n in names])
        return make

    scatter_plan = {"ffn1_dwin": scatter(["ffn1_w_out"]), "ffn1_dx": scatter(["ffn1_w_in"]),
                    "r_dx": scatter(["r_w_out", "gate_w"]), "ffn0_dact": scatter(["r_w_in"]),
                    "ffn0_dwin": scatter(["ffn0_w_out"]), "ffn0_dx": scatter(["ffn0_w_in"]),
                    "m_dwin": scatter(["m_w_out"]), "m_dx": scatter(["m_w_in"])}
    loss_part, grad_x, g, landed = _step(x[0], loss_target[0], p, weights, gather_plan, scatter_plan)

    big = list(shard)
    halves = [_rs_sum(f"rs_sum_{n}", piece[n], gsrc[n], landed[n], jc_idx) for n in big]
    reduced = {n: r.reshape(shard[n].shape) for n, r in zip(big, _rs_share_halves(halves))}

    small_all = ["norm_mix", "norm_ffn", "norm_final", "m_b_if", "m_head_norm", "r_conv_w", "r_conv_b", "r_gate_b", "r_a_param"]
    gsmall = [g["norm_mix"], g["norm_ffn"], g["norm_final"], g["m_b_if"], g["m_head_norm"], g["conv_w"], g["conv_b"],
              g["gate_b"].reshape(RBLOCKS, 2 * W), g["a_param"], loss_part]
    summed = _unpack_rows(_allreduce_small(_pack_rows(gsmall)), [a.shape for a in gsmall])
    loss = summed[-1][0, 0]
    gs = dict(zip(small_all, summed[:-1]))
    gs["norm_final"] = gs["norm_final"].reshape(D)
    gs["m_b_if"] = gs["m_b_if"][:, :m_b_if.shape[1]]
    gs["r_conv_w"] = lax.dynamic_slice_in_dim(gs["r_conv_w"], shard_col * ws, ws, axis=1).reshape(r_conv_w.shape)
    gs["r_conv_b"] = lax.dynamic_slice_in_dim(gs["r_conv_b"], shard_col * ws, ws, axis=1).reshape(r_conv_b.shape)
    gs["r_gate_b"] = lax.dynamic_slice_in_dim(gs["r_gate_b"], shard_col * (2 * W // N_CHIPS), 2 * W // N_CHIPS,
                                              axis=1).reshape(r_gate_b.shape)
    gs["r_a_param"] = lax.dynamic_slice_in_dim(gs["r_a_param"], shard_col * ws, ws, axis=1).reshape(r_a_param.shape)

    grads, delta, new_m, new_v = {}, {}, {}, {}
    layers = {"m_w_in": ["m_w_in"], "m_w_out": ["m_w_out"], "r_w_in": ["r_w_in"], "r_gate_w": ["gate_w"],
              "r_w_out": ["r_w_out"], "ffn_w_in": ["ffn0_w_in", "ffn1_w_in"], "ffn_w_out": ["ffn0_w_out", "ffn1_w_out"]}
    for n, parts in layers.items():
        shp = w[n].shape
        as3d = lambda a: a.reshape((len(parts),) + shard[parts[0]].shape)
        res = _adamw(f"adamw_{n}", as3d(w[n]), [reduced[q] for q in parts], as3d(mom[n]), as3d(var[n]))
        grads[n], delta[n], new_m[n], new_v[n] = (a.reshape(shp) for a in res)

    packs = [_pack_rows([t[n] for n in small_all])[None] for t in (w, gs, mom, var)]
    outs = _adamw("adamw_small", packs[0], [packs[1][0]], packs[2], packs[3])
    for dst, pk in zip((delta, new_m, new_v), outs[1:]):
        for n, a in zip(small_all, _unpack_rows(pk[0], [w[n].shape for n in small_all])):
            dst[n] = a
    for n in small_all:
        grads[n] = gs[n]
    return (loss, grad_x[None], *[grads[n] for n in names], *[delta[n] for n in names],
            *[new_m[n] for n in names], *[new_v[n] for n in names])
```

```python
import functools
import math

import jax
import jax.numpy as jnp
from jax import lax
from jax.experimental import pallas as pl
from jax.experimental.pallas import tpu as pltpu

F32 = jnp.float32
BF16 = jnp.bfloat16

EPS = 1e-6
HEADS = 8
CHUNK = 256
RBLOCKS = 8
CONV_W = 4
R_C = 8.0
LANES = 128
IF_PAD = LANES
VMEM_LIMIT = 52 << 20

ADAM_LR = 0.001
ADAM_B1 = 0.9
ADAM_B2 = 0.999
ADAM_EPS = 1e-08
ADAM_WD = 0.01
ADAM_STEP = 10

_DIMS = {
    "nn": (((1,), (0,)), ((), ())),
    "nt": (((1,), (1,)), ((), ())),
    "tn": (((0,), (0,)), ((), ())),
}


def _dot(a, b, mode="nn"):
    return lax.dot_general(a, b, _DIMS[mode], preferred_element_type=F32)


def _pick(n, pref, mult):
    best = None
    d = mult
    while d <= min(n, pref):
        if n % d == 0:
            best = d
        d += mult
    return best if best is not None else n


def _cparams(sem):
    return pltpu.CompilerParams(dimension_semantics=sem, vmem_limit_bytes=VMEM_LIMIT)


def _a_plain(a, mode, tm, tk):
    if mode == "tn":
        return [(a, (tk, tm), lambda i, j, k: (k, i))], lambda t, i, j, k: t[0]
    return [(a, (tm, tk), lambda i, j, k: (i, k))], lambda t, i, j, k: t[0]


def _a_split_k(a1, a2, tm, tk):
    h = a1.shape[1] // tk
    return ([(a1, (tm, tk), lambda i, j, k: (i, jnp.minimum(k, h - 1))),
             (a2, (tm, tk), lambda i, j, k: (i, jnp.maximum(k - h, 0)))],
            lambda t, i, j, k: jnp.where(k < h, t[0], t[1]))


def _sigmoid(x):
    return 1.0 / (1.0 + jnp.exp(-x))


def _epi_swiglu(acc):
    half = acc.shape[1] // 2
    g, u = acc[:, :half], acc[:, half:]
    return g, u, g * _sigmoid(g) * u


def _b_plain(b, mode, tn, tk):
    if mode == "nt":
        return [(b, (tn, tk), lambda i, j, k: (j, k))], lambda t, i, j, k: t[0]
    return [(b, (tk, tn), lambda i, j, k: (k, j))], lambda t, i, j, k: t[0]


def _b_split_n(b1, b2, tn, tk):
    h = b1.shape[1] // tn
    return ([(b1, (tk, tn), lambda i, j, k: (jnp.where(j < h, k, 0), jnp.minimum(j, h - 1))),
             (b2, (tk, tn), lambda i, j, k: (jnp.where(j < h, 0, k), jnp.maximum(j - h, 0)))],
            lambda t, i, j, k: jnp.where(j < h, t[0], t[1]))


def _b_pair(b, tn, tk):
    off = b.shape[1] // 2 // tn
    return ([(b, (tk, tn), lambda i, j, k: (k, j)), (b, (tk, tn), lambda i, j, k: (k, j + off))],
            lambda t, i, j, k: jnp.concatenate(t, axis=1))


def _mm(name, a_op, b_op, mode, M, N, K, out_defs, *, tm, tn, tk, epi=None, extras=(), carry=None):
    assert M % tm == 0 and N % tn == 0 and K % tk == 0, (name, M, N, K, tm, tn, tk)
    (a_in, a_fn), (b_in, b_fn) = a_op, b_op
    ni, nj, nk = M // tm, N // tn, K // tk
    na, nb, ne, no = len(a_in), len(b_in), len(extras), len(out_defs)
    nci = len(carry.ins) if carry else 0
    nco = len(carry.out_shapes) if carry else 0
    ncs = len(carry.scratch) if carry else 0

    def body(*refs):
        pos = 0

        def take(n):
            nonlocal pos
            r = refs[pos:pos + n]
            pos += n
            return r

        a_refs, b_refs, ex, c_in, outs, c_out = take(na), take(nb), take(ne), take(nci), take(no), take(nco)
        acc_ref = take(1)[0] if nk > 1 else None
        c_scr = take(ncs)
        i, j, k = pl.program_id(0), pl.program_id(1), pl.program_id(2)
        if carry:
            @pl.when((i == 0) & (j == 0) & (k == 0))
            def _():
                carry.start(c_in, c_out, c_scr)

        def finish(acc):
            res = epi(acc, *[e[...] for e in ex]) if epi is not None else (acc,)
            for o_ref, r in zip(outs, res):
                o_ref[...] = r.astype(o_ref.dtype)

        av = a_fn([r[...] for r in a_refs], i, j, k).astype(BF16)
        bv = b_fn([r[...] for r in b_refs], i, j, k).astype(BF16)
        d = _dot(av, bv, mode)
        if nk == 1:
            finish(d)
        else:
            @pl.when(k == 0)
            def _():
                acc_ref[...] = d

            @pl.when(k > 0)
            def _():
                acc_ref[...] += d

            @pl.when(k == nk - 1)
            def _():
                finish(acc_ref[...])
        if carry:
            @pl.when((i == ni - 1) & (j == nj - 1) & (k == nk - 1))
            def _():
                carry.finish(c_in, c_out, c_scr)

    def spec(shape, imap):
        return pl.BlockSpec(shape, imap)

    in_specs = [spec(s, m) for _, s, m in a_in + b_in]
    in_specs += [pl.BlockSpec((tm, w), lambda i, j, k, off=off: (i, j + off)) for _, w, off in extras]
    out_specs = [pl.BlockSpec((tm, c * tn // N), lambda i, j, k: (i, j)) for _, c in out_defs]
    out_shape = [jax.ShapeDtypeStruct((M, c), dt) for dt, c in out_defs]
    scratch = [pltpu.VMEM((tm, tn), F32)] if nk > 1 else []
    operands = [x[0] for x in a_in + b_in] + [e[0] for e in extras]
    if carry:
        in_specs += [_HBM] * nci
        out_specs += [_HBM] * nco
        out_shape += list(carry.out_shapes)
        scratch += list(carry.scratch)
        operands += list(carry.ins)
    sem = ("arbitrary",) * 3 if carry else ("parallel", "parallel", "arbitrary")
    res = pl.pallas_call(
        body, name=name, grid=(ni, nj, nk), in_specs=in_specs, out_specs=out_specs, out_shape=out_shape,
        scratch_shapes=scratch, compiler_params=_cparams(sem),
    )(*operands)
    return res[:no], res[no:]


def _epi_resid(acc, resid):
    return (resid + acc,)


def _epi_swiglu_bwd(da, g, u):
    g = g.astype(F32)
    u = u.astype(F32)
    sig = _sigmoid(g)
    return da * u * (sig * (1.0 + g * (1.0 - sig))), da * (g * sig)


def _rms_fwd(name, x, g):
    T, D = x.shape
    tr = _pick(T, 512, 16)

    def body(x_ref, g_ref, hn_ref, r_ref):
        xv = x_ref[...]
        r = lax.rsqrt(jnp.mean(xv * xv, axis=-1, keepdims=True) + EPS)
        hn_ref[...] = (xv * r * g_ref[...]).astype(BF16)
        r_ref[...] = r

    return pl.pallas_call(
        body, name=name, grid=(T // tr,),
        in_specs=[pl.BlockSpec((tr, D), lambda i: (i, 0)), pl.BlockSpec((1, D), lambda i: (0, 0))],
        out_specs=[pl.BlockSpec((tr, D), lambda i: (i, 0)), pl.BlockSpec((tr, 1), lambda i: (i, 0))],
        out_shape=[jax.ShapeDtypeStruct((T, D), BF16), jax.ShapeDtypeStruct((T, 1), F32)],
        compiler_params=_cparams(("parallel",)),
    )(x, g)


def _rms_bwd(name, x, r, dy, g, dres):
    T, D = x.shape
    tr = _pick(T, 512, 16)

    def body(x_ref, r_ref, dy_ref, g_ref, dres_ref, dh_ref, dhb_ref, dg_ref):
        xh = x_ref[...] * r_ref[...]
        dy_v = dy_ref[...]
        dyg = dy_v * g_ref[...]
        m = jnp.mean(dyg * xh, axis=-1, keepdims=True)
        dh = dres_ref[...] + r_ref[...] * (dyg - xh * m)
        dh_ref[...] = dh
        dhb_ref[...] = dh.astype(BF16)

        @pl.when(pl.program_id(0) == 0)
        def _():
            dg_ref[...] = jnp.zeros_like(dg_ref)

        dg_ref[...] += jnp.sum(dy_v * xh, axis=0, keepdims=True)

    row = pl.BlockSpec((tr, D), lambda i: (i, 0))
    vec = pl.BlockSpec((1, D), lambda i: (0, 0))
    return pl.pallas_call(
        body, name=name, grid=(T // tr,),
        in_specs=[row, pl.BlockSpec((tr, 1), lambda i: (i, 0)), row, vec, row],
        out_specs=[row, row, vec],
        out_shape=[jax.ShapeDtypeStruct((T, D), F32), jax.ShapeDtypeStruct((T, D), BF16),
                   jax.ShapeDtypeStruct((1, D), F32)],
        compiler_params=_cparams(("arbitrary",)),
    )(x, r, dy, g, dres)


def _loss_head(x, g, tgt):
    T, D = x.shape
    tr = _pick(T, 512, 16)

    def body(x_ref, g_ref, t_ref, loss_ref, dh_ref, dhb_ref, dg_ref):
        xv = x_ref[...]
        gv = g_ref[...]
        r = lax.rsqrt(jnp.mean(xv * xv, axis=-1, keepdims=True) + EPS)
        xh = xv * r
        e = xh * gv - t_ref[...]
        part = 0.5 * jnp.sum(jnp.mean(e * e, axis=-1, keepdims=True), axis=0, keepdims=True)
        dy = e * (1.0 / D)
        dyg = dy * gv
        m = jnp.mean(dyg * xh, axis=-1, keepdims=True)
        dh = r * (dyg - xh * m)
        dh_ref[...] = dh
        dhb_ref[...] = dh.astype(BF16)

        @pl.when(pl.program_id(0) == 0)
        def _():
            dg_ref[...] = jnp.zeros_like(dg_ref)
            loss_ref[...] = jnp.zeros_like(loss_ref)

        dg_ref[...] += jnp.sum(dy * xh, axis=0, keepdims=True)
        loss_ref[...] += jnp.broadcast_to(part, loss_ref.shape)

    row = pl.BlockSpec((tr, D), lambda i: (i, 0))
    vec = pl.BlockSpec((1, D), lambda i: (0, 0))
    return pl.pallas_call(
        body, name="loss_head", grid=(T // tr,),
        in_specs=[row, vec, row],
        out_specs=[pl.BlockSpec((1, LANES), lambda i: (0, 0)), row, row, vec],
        out_shape=[jax.ShapeDtypeStruct((1, LANES), F32), jax.ShapeDtypeStruct((T, D), F32),
                   jax.ShapeDtypeStruct((T, D), BF16), jax.ShapeDtypeStruct((1, D), F32)],
        compiler_params=_cparams(("arbitrary",)),
    )(x, g, tgt)


def _mlstm_gates(proj, b_if_pad, D):
    T = proj.shape[0]
    tr = _pick(T, 512, CHUNK)

    def body(p_ref, b_ref, o_ref):
        z = p_ref[...] + b_ref[...]
        lane = lax.broadcasted_iota(jnp.int32, z.shape, 1)
        row = lax.broadcasted_iota(jnp.int32, z.shape, 0) % CHUNK
        lf = jnp.where((lane >= HEADS) & (lane < 2 * HEADS),
                       jnp.minimum(z, 0.0) - jnp.log(1.0 + jnp.exp(-jnp.abs(z))), 0.0)
        c = lf
        d = 1
        while d < CHUNK:
            c = c + jnp.where(row >= d, pltpu.roll(c, d, 0), 0.0)
            d *= 2
        o_ref[...] = jnp.where(lane < HEADS, z, 0.0) + lf + pltpu.roll(c, HEADS, 1)

    return pl.pallas_call(
        body, name="mlstm_gates", grid=(T // tr,),
        in_specs=[pl.BlockSpec((tr, IF_PAD), lambda i: (i, 3 * D // IF_PAD)),
                  pl.BlockSpec((1, IF_PAD), lambda i: (0, 0))],
        out_specs=pl.BlockSpec((tr, IF_PAD), lambda i: (i, 0)),
        out_shape=jax.ShapeDtypeStruct((T, IF_PAD), F32),
        compiler_params=_cparams(("parallel",)),
    )(proj, b_if_pad)


def _mlstm_chunk_common(qf, kf, vf, gc, gr, h, c_prev, n_prev, m_prev, causal):
    L = CHUNK
    qb = qf.astype(BF16)
    kb = kf.astype(BF16)
    vb = vf.astype(BF16)
    i_col = gc[:, h:h + 1]
    b_col = gc[:, 2 * HEADS + h:2 * HEADS + h + 1]
    i_row = gr[h:h + 1, :]
    b_row = gr[2 * HEADS + h:2 * HEADS + h + 1, :]
    dmat = jnp.where(causal, b_col - b_row + i_row, -jnp.inf)
    m_inter = b_col + m_prev
    m_t = jnp.maximum(m_inter, jnp.max(dmat, axis=-1, keepdims=True))
    w = jnp.exp(dmat - m_t)
    qk = _dot(qb, kb, "nt")
    s = qk * w
    sc = jnp.exp(m_inter - m_t)
    cb = c_prev.astype(BF16)
    num_inter = _dot(qb, cb)
    nb = n_prev.astype(BF16).astype(F32)
    qn = jnp.sum(qb.astype(F32) * nb, axis=-1, keepdims=True)
    num = _dot(s.astype(BF16), vb) + sc * num_inter
    den = jnp.sum(s, axis=-1, keepdims=True) + sc * qn
    e_m = jnp.exp(-m_t)
    nrm = jnp.maximum(jnp.abs(den), e_m)
    hh = num / nrm
    b_last = b_row[:, L - 1:L]
    g_row = b_last - b_row + i_row
    g_col = b_last - b_col + i_col
    m_new = jnp.maximum(b_last + m_prev, jnp.max(g_row, axis=-1, keepdims=True))
    wk = jnp.exp(g_col - m_new)
    decay = jnp.exp(b_last + m_prev - m_new)
    kw = kf * wk
    return dict(qb=qb, kb=kb, vb=vb, w=w, s=s, sc=sc, cb=cb, nb=nb, qn=qn, num_inter=num_inter,
                den=den, e_m=e_m, nrm=nrm, hh=hh, m_new=m_new, wk=wk, decay=decay, kw=kw)


def _mlstm_specs(T, D):
    qk = D // 2
    row = lambda w, j: pl.BlockSpec((CHUNK, w), lambda c, w=w, j=j: (c, j))
    return [row(qk, 0), row(qk, 1), row(D, 1), row(D, 2)]


def _mlstm_fwd(proj, gcol, grow, head_norm, carry=None):
    T = proj.shape[0]
    D = head_norm.shape[1]
    nc = T // CHUNK
    dk, dv = D // 2 // HEADS, D // HEADS
    kscale = dk ** -0.5
    L = CHUNK
    nci = len(carry.ins) if carry else 0
    nco = len(carry.out_shapes) if carry else 0

    def body(*refs):
        q_ref, k_ref, v_ref, o_ref, gc_ref, gr_ref, hn_ref = refs[:7]
        c_in = refs[7:7 + nci]
        hg_ref, cs_ref, ns_ref, ms_ref = refs[7 + nci:11 + nci]
        c_out = refs[11 + nci:11 + nci + nco]
        c_s, n_s, m_s = refs[11 + nci + nco:14 + nci + nco]
        c_scr = refs[14 + nci + nco:]

        @pl.when(pl.program_id(0) == 0)
        def _():
            c_s[...] = jnp.zeros_like(c_s)
            n_s[...] = jnp.zeros_like(n_s)
            m_s[...] = jnp.zeros_like(m_s)
            if carry:
                carry.start(c_in, c_out, c_scr)

        cs_ref[0] = c_s[...]
        ns_ref[0] = n_s[...]
        ms_ref[0] = m_s[...]
        causal = lax.broadcasted_iota(jnp.int32, (L, L), 1) <= lax.broadcasted_iota(jnp.int32, (L, L), 0)
        gc = gc_ref[...]
        gr = gr_ref[0]
        for h in range(HEADS):
            qf = q_ref[:, h * dk:(h + 1) * dk]
            kf = k_ref[:, h * dk:(h + 1) * dk] * kscale
            vf = v_ref[:, h * dv:(h + 1) * dv]
            m_prev = m_s[h:h + 1, 0:1]
            f = _mlstm_chunk_common(qf, kf, vf, gc, gr, h, c_s[h], n_s[h:h + 1, :], m_prev, causal)
            c_s[h] = f["decay"] * c_s[h] + _dot(f["kw"].astype(BF16), f["vb"], "tn")
            n_s[h:h + 1, :] = f["decay"] * n_s[h:h + 1, :] + jnp.sum(f["kw"], axis=0, keepdims=True)
            m_s[h:h + 1, :] = jnp.broadcast_to(f["m_new"], (1, LANES))
            hh = f["hh"]
            rs = lax.rsqrt(jnp.mean(hh * hh, axis=-1, keepdims=True) + EPS)
            sl = slice(h * dv, (h + 1) * dv)
            hg_ref[:, sl] = (hh * rs * hn_ref[:, sl] * _sigmoid(o_ref[:, sl])).astype(BF16)
        if carry:
            @pl.when(pl.program_id(0) == nc - 1)
            def _():
                carry.finish(c_in, c_out, c_scr)

    res = pl.pallas_call(
        body, name="mlstm_fwd", grid=(nc,),
        in_specs=_mlstm_specs(T, D) + [
            pl.BlockSpec((L, IF_PAD), lambda c: (c, 0)),
            pl.BlockSpec((1, 3 * HEADS, L), lambda c: (c, 0, 0)),
            pl.BlockSpec((1, D), lambda c: (0, 0)),
        ] + [_HBM] * nci,
        out_specs=[
            pl.BlockSpec((L, D), lambda c: (c, 0)),
            pl.BlockSpec((1, HEADS, dk, dv), lambda c: (c, 0, 0, 0)),
            pl.BlockSpec((1, HEADS, dk), lambda c: (c, 0, 0)),
            pl.BlockSpec((1, HEADS, LANES), lambda c: (c, 0, 0)),
        ] + [_HBM] * nco,
        out_shape=[
            jax.ShapeDtypeStruct((T, D), BF16),
            jax.ShapeDtypeStruct((nc, HEADS, dk, dv), F32),
            jax.ShapeDtypeStruct((nc, HEADS, dk), F32),
            jax.ShapeDtypeStruct((nc, HEADS, LANES), F32),
        ] + (list(carry.out_shapes) if carry else []),
        scratch_shapes=[pltpu.VMEM((HEADS, dk, dv), F32), pltpu.VMEM((HEADS, dk), F32),
                        pltpu.VMEM((HEADS, LANES), F32)] + (list(carry.scratch) if carry else []),
        compiler_params=_cparams(("arbitrary",)),
    )(proj, proj, proj, proj, gcol, grow, head_norm, *(carry.ins if carry else []))
    return res[:4], res[4:]


def _mlstm_bwd(proj, gcol, grow, head_norm, dhg, cs, ns, ms):
    T = proj.shape[0]
    D = head_norm.shape[1]
    nc = T // CHUNK
    dk, dv = D // 2 // HEADS, D // HEADS
    qkw = D // 2
    kscale = dk ** -0.5
    L = CHUNK
    pw = 3 * D + IF_PAD

    def body(q_ref, k_ref, v_ref, o_ref, gc_ref, gr_ref, hn_ref, dhg_ref, cs_ref, ns_ref, ms_ref,
             dp_ref, dgain_ref, dbif_ref, dc_s, dn_s):
        @pl.when(pl.program_id(0) == 0)
        def _():
            dc_s[...] = jnp.zeros_like(dc_s)
            dn_s[...] = jnp.zeros_like(dn_s)
            dgain_ref[...] = jnp.zeros_like(dgain_ref)
            dbif_ref[...] = jnp.zeros_like(dbif_ref)

        rowl = lax.broadcasted_iota(jnp.int32, (L, L), 0)
        coll = lax.broadcasted_iota(jnp.int32, (L, L), 1)
        causal = coll <= rowl
        eye = coll == rowl
        lane = lax.broadcasted_iota(jnp.int32, (L, IF_PAD), 1)
        rowg = lax.broadcasted_iota(jnp.int32, (L, IF_PAD), 0)
        gc = gc_ref[...]
        gr = gr_ref[0]
        dgate = jnp.zeros((L, IF_PAD), F32)
        for h in range(HEADS):
            qf = q_ref[:, h * dk:(h + 1) * dk]
            kf = k_ref[:, h * dk:(h + 1) * dk] * kscale
            vf = v_ref[:, h * dv:(h + 1) * dv]
            c_prev = cs_ref[0, h]
            n_prev = ns_ref[0, h:h + 1, :]
            m_prev = ms_ref[0, h:h + 1, 0:1]
            f = _mlstm_chunk_common(qf, kf, vf, gc, gr, h, c_prev, n_prev, m_prev, causal)
            qb, kb, vb, w, s, sc = f["qb"], f["kb"], f["vb"], f["w"], f["s"], f["sc"]
            hh, nrm, den = f["hh"], f["nrm"], f["den"]
            sl = slice(h * dv, (h + 1) * dv)
            dhg_v = dhg_ref[:, sl]
            sig = _sigmoid(o_ref[:, sl])
            gain = hn_ref[:, sl]
            rs = lax.rsqrt(jnp.mean(hh * hh, axis=-1, keepdims=True) + EPS)
            hn_v = hh * rs
            d_o = dhg_v * hn_v * gain * sig * (1.0 - sig)
            dgain_ref[:, sl] += jnp.sum(dhg_v * hn_v * sig, axis=0, keepdims=True)
            dhn = dhg_v * gain * sig
            dh = rs * (dhn - hn_v * jnp.mean(dhn * hn_v, axis=-1, keepdims=True))
            dnum = dh / nrm
            ddn = -jnp.sum(dh * hh, axis=-1, keepdims=True) / nrm
            dden = jnp.where(jnp.abs(den) > f["e_m"], jnp.where(den > 0.0, ddn, -ddn), 0.0)
            dnum_b = dnum.astype(BF16)
            d_s = _dot(dnum_b, vb, "nt") + dden
            dqk = (d_s * w).astype(BF16)
            p = d_s * s
            dcn = dc_s[h]
            dcb = dcn.astype(BF16)
            dnn = dn_s[h:h + 1, :]
            kwb = f["kw"].astype(BF16)
            d_v = _dot(s.astype(BF16), dnum_b, "tn") + _dot(kwb, dcb)
            dkw = _dot(vb, dcb, "nt") + dnn
            d_q = _dot(dqk, kb) + sc * (_dot(dnum_b, f["cb"], "nt") + dden * f["nb"])
            d_kf = _dot(dqk, qb, "tn") + f["wk"] * dkw
            u_col = f["wk"] * jnp.sum(kf * dkw, axis=-1, keepdims=True)
            z = f["decay"] * (jnp.sum(jnp.sum(c_prev * dcn, axis=-1, keepdims=True), axis=0, keepdims=True)
                              + jnp.sum(n_prev * dnn, axis=-1, keepdims=True))
            r_col = sc * (jnp.sum(dnum * f["num_inter"], axis=-1, keepdims=True) + dden * f["qn"])
            rowsum_p = jnp.sum(p, axis=-1, keepdims=True)
            colsum_p = jnp.sum(p, axis=0, keepdims=True)
            colsum_col = jnp.sum(jnp.where(eye, colsum_p, 0.0), axis=-1, keepdims=True)
            di_col = colsum_col + u_col
            db_col = rowsum_p + r_col - colsum_col - u_col
            db_last = jnp.sum(u_col, axis=0, keepdims=True) + z
            qs = (qb.astype(F32) * sc)
            dc_s[h] = f["decay"] * dcn + _dot(qs.astype(BF16), dnum_b, "tn")
            dn_s[h:h + 1, :] = f["decay"] * dnn + jnp.sum(qs * dden, axis=0, keepdims=True)
            dgate = jnp.where(lane == h, di_col, dgate)
            dgate = jnp.where(lane == 2 * HEADS + h,
                              db_col + jnp.where(rowg == L - 1, db_last, 0.0), dgate)
            dp_ref[:, h * dk:(h + 1) * dk] = d_q.astype(BF16)
            dp_ref[:, qkw + h * dk:qkw + (h + 1) * dk] = (d_kf * kscale).astype(BF16)
            dp_ref[:, D + h * dv:D + (h + 1) * dv] = d_v.astype(BF16)
            dp_ref[:, 2 * D + h * dv:2 * D + (h + 1) * dv] = d_o.astype(BF16)
        rc = jnp.where(lane >= 2 * HEADS, dgate, 0.0)
        d = 1
        while d < L:
            rc = rc + jnp.where(rowg < L - d, pltpu.roll(rc, L - d, 0), 0.0)
            d *= 2
        dlf = pltpu.roll(rc, IF_PAD - HEADS, 1)
        dif = jnp.where(lane < HEADS, dgate, 0.0) + jnp.where(
            (lane >= HEADS) & (lane < 2 * HEADS), dlf * (1.0 - jnp.exp(gc)), 0.0)
        dp_ref[:, 3 * D:3 * D + IF_PAD] = dif.astype(BF16)
        dbif_ref[...] += jnp.sum(dif, axis=0, keepdims=True)

    rev = lambda c: nc - 1 - c
    specs = [pl.BlockSpec(s.block_shape, (lambda c, f=s.index_map: f(rev(c)))) for s in _mlstm_specs(T, D)]
    return pl.pallas_call(
        body, name="mlstm_bwd", grid=(nc,),
        in_specs=specs + [
            pl.BlockSpec((L, IF_PAD), lambda c: (rev(c), 0)),
            pl.BlockSpec((1, 3 * HEADS, L), lambda c: (rev(c), 0, 0)),
            pl.BlockSpec((1, D), lambda c: (0, 0)),
            pl.BlockSpec((L, D), lambda c: (rev(c), 0)),
            pl.BlockSpec((1, HEADS, dk, dv), lambda c: (rev(c), 0, 0, 0)),
            pl.BlockSpec((1, HEADS, dk), lambda c: (rev(c), 0, 0)),
            pl.BlockSpec((1, HEADS, LANES), lambda c: (rev(c), 0, 0)),
        ],
        out_specs=[
            pl.BlockSpec((L, pw), lambda c: (rev(c), 0)),
            pl.BlockSpec((1, D), lambda c: (0, 0)),
            pl.BlockSpec((1, IF_PAD), lambda c: (0, 0)),
        ],
        out_shape=[
            jax.ShapeDtypeStruct((T, pw), BF16),
            jax.ShapeDtypeStruct((1, D), F32),
            jax.ShapeDtypeStruct((1, IF_PAD), F32),
        ],
        scratch_shapes=[pltpu.VMEM((HEADS, dk, dv), F32), pltpu.VMEM((HEADS, dk), F32)],
        compiler_params=_cparams(("arbitrary",)),
    )(proj, proj, proj, proj, gcol, grow, head_norm, dhg, cs, ns, ms)


_GELU_C = math.sqrt(2.0 / math.pi)


def _log_sigmoid(x):
    return jnp.minimum(x, 0.0) - jnp.log(1.0 + jnp.exp(-jnp.abs(x)))


def _rglru_recompute(rec_ref, halo_ref, cw_ref, cb_ref, gw_ref, gb_ref, ap_ref, first, W, Lt):
    x = rec_ref[...]
    halo = jnp.where(first, 0.0, halo_ref[...])
    xe = jnp.concatenate([halo, x], axis=0)
    xs = [pltpu.roll(xe, d, 0)[8:8 + Lt] for d in (3, 2, 1)] + [x]
    cw = cw_ref[...]
    rec_c = cb_ref[...] + cw[0:1] * xs[0] + cw[1:2] * xs[1] + cw[2:3] * xs[2] + cw[3:4] * xs[3]
    gates = _dot(rec_c.astype(BF16), gw_ref[0]) + gb_ref[0]
    r = _sigmoid(gates[:, :W])
    ig = _sigmoid(gates[:, W:])
    lsa = _log_sigmoid(ap_ref[...])
    log_a = R_C * r * lsa
    a = jnp.exp(log_a)
    mult = jnp.sqrt(1.0 - jnp.exp(2.0 * log_a))
    return dict(xs=xs, cw=cw, rec_c=rec_c, r=r, ig=ig, lsa=lsa, a=a, mult=mult)


def _rglru_specs(T, D, Lt, tmap):
    W = D // RBLOCKS
    hb = Lt // 8
    return [
        pl.BlockSpec((Lt, W), lambda g, i: (tmap(i), g)),
        pl.BlockSpec((Lt, W), lambda g, i: (tmap(i), RBLOCKS + g)),
        pl.BlockSpec((8, W), lambda g, i: (jnp.maximum(tmap(i) * hb - 1, 0), RBLOCKS + g)),
        pl.BlockSpec((CONV_W, W), lambda g, i: (0, g)),
        pl.BlockSpec((1, W), lambda g, i: (0, g)),
        pl.BlockSpec((1, W, 2 * W), lambda g, i: (g, 0, 0)),
        pl.BlockSpec((1, 1, 2 * W), lambda g, i: (g, 0, 0)),
        pl.BlockSpec((1, W), lambda g, i: (0, g)),
    ]


def _rglru_fwd(pr, conv_w, conv_b, gate_w, gate_b, a_param):
    T = pr.shape[0]
    D = pr.shape[1] // 2
    W = D // RBLOCKS
    Lt = _pick(T, 256, 8)
    nt = T // Lt

    def body(gate_ref, rec_ref, halo_ref, cw_ref, cb_ref, gw_ref, gb_ref, ap_ref, y_ref, hs_ref, carry):
        i = pl.program_id(1)

        @pl.when(i == 0)
        def _():
            carry[...] = jnp.zeros_like(carry)

        f = _rglru_recompute(rec_ref, halo_ref, cw_ref, cb_ref, gw_ref, gb_ref, ap_ref, i == 0, W, Lt)
        row = lax.broadcasted_iota(jnp.int32, (Lt, W), 0)
        a_c = f["a"]
        u_c = f["mult"] * (f["ig"] * f["rec_c"])
        d = 1
        while d < Lt:
            msk = row >= d
            u_c = jnp.where(msk, a_c * pltpu.roll(u_c, d, 0) + u_c, u_c)
            a_c = jnp.where(msk, a_c * pltpu.roll(a_c, d, 0), a_c)
            d *= 2
        h = u_c + a_c * carry[0:1, :]
        carry[0:1, :] = h[Lt - 1:Lt, :]
        hs_ref[...] = h
        gb = gate_ref[...]
        t = jnp.tanh(_GELU_C * (gb + 0.044715 * gb * gb * gb))
        y_ref[...] = (0.5 * gb * (1.0 + t) * h).astype(BF16)

    blk = pl.BlockSpec((Lt, W), lambda g, i: (i, g))
    return pl.pallas_call(
        body, name="rglru_fwd", grid=(RBLOCKS, nt),
        in_specs=_rglru_specs(T, D, Lt, lambda i: i),
        out_specs=[blk, blk],
        out_shape=[jax.ShapeDtypeStruct((T, D), BF16), jax.ShapeDtypeStruct((T, D), F32)],
        scratch_shapes=[pltpu.VMEM((8, W), F32)],
        compiler_params=_cparams(("parallel", "arbitrary")),
    )(pr, pr, pr, conv_w, conv_b, gate_w, gate_b, a_param)


def _rglru_bwd(pr, hs, dy, conv_w, conv_b, gate_w, gate_b, a_param):
    T = pr.shape[0]
    D = pr.shape[1] // 2
    W = D // RBLOCKS
    Lt = _pick(T, 256, 8)
    nt = T // Lt
    hb = Lt // 8
    tmap = lambda i: nt - 1 - i

    def body(gate_ref, rec_ref, halo_ref, cw_ref, cb_ref, gw_ref, gb_ref, ap_ref, hs_ref, hh_ref, dy_ref,
             dgate_ref, drec_ref, dcw_ref, dcb_ref, dgw_ref, dgb_ref, dap_ref, lam_s, drc_s):
        i = pl.program_id(1)
        ti = nt - 1 - i

        @pl.when(i == 0)
        def _():
            lam_s[...] = jnp.zeros_like(lam_s)
            drc_s[...] = jnp.zeros_like(drc_s)
            dcw_ref[...] = jnp.zeros_like(dcw_ref)
            dcb_ref[...] = jnp.zeros_like(dcb_ref)
            dgw_ref[...] = jnp.zeros_like(dgw_ref)
            dgb_ref[...] = jnp.zeros_like(dgb_ref)
            dap_ref[...] = jnp.zeros_like(dap_ref)

        f = _rglru_recompute(rec_ref, halo_ref, cw_ref, cb_ref, gw_ref, gb_ref, ap_ref, ti == 0, W, Lt)
        a, mult, ig, r, rec_c, lsa, xs, cw = (f[k] for k in ("a", "mult", "ig", "r", "rec_c", "lsa", "xs", "cw"))
        row = lax.broadcasted_iota(jnp.int32, (Lt, W), 0)
        gb = gate_ref[...]
        t = jnp.tanh(_GELU_C * (gb + 0.044715 * gb * gb * gb))
        gel = 0.5 * gb * (1.0 + t)
        dgel = 0.5 * (1.0 + t) + 0.5 * gb * (1.0 - t * t) * _GELU_C * (1.0 + 3.0 * 0.044715 * gb * gb)
        h = hs_ref[...]
        h_first = jnp.where(ti == 0, 0.0, hh_ref[7:8, :])
        h_prev = jnp.where(row == 0, h_first, pltpu.roll(h, 1, 0))
        dy_v = dy_ref[...]
        d_gb = dy_v * h * dgel
        c_c = jnp.where(row < Lt - 1, pltpu.roll(a, Lt - 1, 0), 0.0)
        g_c = dy_v * gel + jnp.where(row == Lt - 1, lam_s[0:1, :], 0.0)
        d = 1
        while d < Lt:
            msk = row < Lt - d
            g_c = jnp.where(msk, c_c * pltpu.roll(g_c, Lt - d, 0) + g_c, g_c)
            c_c = jnp.where(msk, c_c * pltpu.roll(c_c, Lt - d, 0), c_c)
            d *= 2
        lam = g_c
        lam_s[0:1, :] = (a * lam)[0:1, :]
        d_mult = lam * ig * rec_c
        d_loga = lam * h_prev * a - d_mult * (a * a) / mult
        d_ig = lam * mult * rec_c
        d_rec = lam * mult * ig
        d_r = d_loga * (R_C * lsa)
        dap_ref[...] += jnp.sum(d_loga * (R_C * r), axis=0, keepdims=True) * (1.0 - jnp.exp(lsa))
        dgates = jnp.concatenate([d_r * r * (1.0 - r), d_ig * ig * (1.0 - ig)], axis=1)
        dgb_ref[0] += jnp.sum(dgates, axis=0, keepdims=True)
        dgates_b = dgates.astype(BF16)
        dgw_ref[0] += _dot(rec_c.astype(BF16), dgates_b, "tn")
        d_rec = d_rec + _dot(dgates_b, gw_ref[0], "nt")
        dcb_ref[...] += jnp.sum(d_rec, axis=0, keepdims=True)
        dcw_ref[...] += jnp.concatenate(
            [jnp.sum(d_rec * xs[j], axis=0, keepdims=True) for j in range(CONV_W)], axis=0)
        ext = jnp.concatenate([d_rec, drc_s[...]], axis=0)
        up = lambda s: pltpu.roll(ext, Lt + 8 - s, 0)[:Lt]
        d_x = cw[3:4] * d_rec + cw[2:3] * up(1) + cw[1:2] * up(2) + cw[0:1] * up(3)
        drc_s[...] = d_rec[0:8, :]
        dgate_ref[...] = d_gb.astype(BF16)
        drec_ref[...] = d_x.astype(BF16)

    blk = pl.BlockSpec((Lt, W), lambda g, i: (tmap(i), g))
    return pl.pallas_call(
        body, name="rglru_bwd", grid=(RBLOCKS, nt),
        in_specs=_rglru_specs(T, D, Lt, tmap) + [
            blk,
            pl.BlockSpec((8, W), lambda g, i: (jnp.maximum(tmap(i) * hb - 1, 0), g)),
            blk,
        ],
        out_specs=[
            blk,
            blk,
            pl.BlockSpec((CONV_W, W), lambda g, i: (0, g)),
            pl.BlockSpec((1, W), lambda g, i: (0, g)),
            pl.BlockSpec((1, W, 2 * W), lambda g, i: (g, 0, 0)),
            pl.BlockSpec((1, 1, 2 * W), lambda g, i: (g, 0, 0)),
            pl.BlockSpec((1, W), lambda g, i: (0, g)),
        ],
        out_shape=[
            jax.ShapeDtypeStruct((T, D), BF16),
            jax.ShapeDtypeStruct((T, D), BF16),
            jax.ShapeDtypeStruct((CONV_W, D), F32),
            jax.ShapeDtypeStruct((1, D), F32),
            jax.ShapeDtypeStruct((RBLOCKS, W, 2 * W), F32),
            jax.ShapeDtypeStruct((RBLOCKS, 1, 2 * W), F32),
            jax.ShapeDtypeStruct((1, D), F32),
        ],
        scratch_shapes=[pltpu.VMEM((8, W), F32), pltpu.VMEM((8, W), F32)],
        compiler_params=_cparams(("parallel", "arbitrary")),
    )(pr, pr, pr, conv_w, conv_b, gate_w, gate_b, a_param, hs, hs, dy)


def _adamw(name, w, gs, m, v):
    L, R, C = w.shape
    if R % 8 == 0:
        tr, tc = _pick(R, max(8, (1 << 19) // C), 8), C
    else:
        tr, tc = R, _pick(C, max(LANES, (1 << 19) // R), LANES)

    def body(*refs):
        w_ref, m_ref, v_ref = refs[:3]
        g_refs = refs[3:3 + L]
        go_ref, d_ref, nm_ref, nv_ref = refs[3 + L:]
        l = pl.program_id(0)
        gv = g_refs[0][...]
        for q in range(1, L):
            gv = jnp.where(l == q, g_refs[q][...], gv)
        mv = ADAM_B1 * m_ref[0] + (1.0 - ADAM_B1) * gv
        vv = ADAM_B2 * v_ref[0] + (1.0 - ADAM_B2) * (gv * gv)
        m_hat = mv / (1.0 - ADAM_B1 ** ADAM_STEP)
        v_hat = vv / (1.0 - ADAM_B2 ** ADAM_STEP)
        go_ref[0] = gv
        d_ref[0] = -ADAM_LR * (m_hat / (jnp.sqrt(v_hat) + ADAM_EPS) + ADAM_WD * w_ref[0])
        nm_ref[0] = mv
        nv_ref[0] = vv

    blk = pl.BlockSpec((1, tr, tc), lambda l, i, j: (l, i, j))
    g_specs = [pl.BlockSpec((tr, tc), lambda l, i, j, q=q: (jnp.where(l == q, i, 0), jnp.where(l == q, j, 0)))
               for q in range(L)]
    sds = jax.ShapeDtypeStruct((L, R, C), F32)
    return pl.pallas_call(
        body, name=name, grid=(L, R // tr, C // tc),
        in_specs=[blk] * 3 + g_specs, out_specs=[blk] * 4, out_shape=[sds] * 4,
        compiler_params=_cparams(("parallel", "parallel", "parallel")),
    )(w, m, v, *gs)


def _step(x, tgt, p, weights, gather_plan, scatter_plan):
    T, D = x.shape
    pw = 3 * D + IF_PAD
    nc = T // CHUNK
    tm = _pick(T, 1024, 128)
    tkt = _pick(T, 1024, 128)
    td = _pick(D, 1024, 128)
    vec = lambda a, l: a[l:l + 1]
    w = dict(weights)
    g, landed = {}, {}

    def mm(name, a_op, b_op, mode, M, N, K, out_defs, **kw):
        carry = None
        if name in gather_plan:
            carry = gather_plan[name]
        elif name in scatter_plan:
            carry = scatter_plan[name](g)
        outs, extra = _mm(name, a_op, b_op, mode, M, N, K, out_defs, carry=carry, **kw)
        if name in gather_plan:
            for n, a in zip(carry.names, extra):
                w[n] = carry.post[n](a) if n in carry.post else a
        elif carry is not None:
            landed.update(zip(carry.names, extra))
        return outs

    def ffn_fwd(l, h_in):
        dff = w[f"ffn{l}_w_out"].shape[0]
        hn, r = _rms_fwd(f"ffn{l}_norm", h_in, vec(p["norm_ffn"], l))
        tn = _pick(dff, 512, 128)
        gt, ut, act = mm(f"ffn{l}_in", _a_plain(hn, "nn", tm, D), _b_pair(w[f"ffn{l}_w_in"], tn, D), "nn",
                         T, dff, D, [(BF16, dff)] * 3, tm=tm, tn=tn, tk=D, epi=_epi_swiglu)
        tk = _pick(dff, 2816, 128)
        (h_out,) = mm(f"ffn{l}_out", _a_plain(act, "nn", tm, tk), _b_plain(w[f"ffn{l}_w_out"], "nn", td, tk), "nn",
                      T, D, dff, [(F32, D)], tm=tm, tn=td, tk=tk, epi=_epi_resid, extras=[(h_in, td, 0)])
        return h_out, (hn, r, gt, ut, act)

    def ffn_bwd(l, h_in, saved, dh, dhb):
        hn, r, gt, ut, act = saved
        dff = w[f"ffn{l}_w_out"].shape[0]
        tn = _pick(dff, 512, 128)
        dg, du = mm(f"ffn{l}_dact", _a_plain(dhb, "nt", tm, D), _b_plain(w[f"ffn{l}_w_out"], "nt", tn, D), "nt",
                    T, dff, D, [(BF16, dff), (BF16, dff)], tm=tm, tn=tn, tk=D, epi=_epi_swiglu_bwd,
                    extras=[(gt, tn, 0), (ut, tn, 0)])
        tmf = _pick(dff, 1024, 128)
        (g[f"ffn{l}_w_out"],) = mm(f"ffn{l}_dwout", _a_plain(act, "tn", tmf, tkt), _b_plain(dhb, "tn", td, tkt), "tn",
                                   dff, D, T, [(BF16, D)], tm=tmf, tn=td, tk=tkt)
        tn2 = _pick(dff, 1024, 128)
        (g[f"ffn{l}_w_in"],) = mm(f"ffn{l}_dwin", _a_plain(hn, "tn", td, tkt), _b_split_n(dg, du, tn2, tkt), "tn",
                                  D, 2 * dff, T, [(BF16, 2 * dff)], tm=td, tn=tn2, tk=tkt)
        tk2 = _pick(dff, 2048, 128)
        (dhn,) = mm(f"ffn{l}_dx", _a_split_k(dg, du, tm, tk2), _b_plain(w[f"ffn{l}_w_in"], "nt", td, tk2), "nt",
                    T, D, 2 * dff, [(F32, D)], tm=tm, tn=td, tk=tk2)
        dh, dhb, dgn = _rms_bwd(f"ffn{l}_dnorm", h_in, r, dhn, vec(p["norm_ffn"], l), dh)
        return dh, dhb, dgn

    h0 = x
    hn0, r0 = _rms_fwd("mix0_norm", h0, vec(p["norm_mix"], 0))
    tnp = _pick(pw, 1024, 128)
    (proj,) = mm("m_proj", _a_plain(hn0, "nn", tm, D), _b_plain(w["m_w_in"], "nn", tnp, D), "nn",
                 T, pw, D, [(F32, pw)], tm=tm, tn=tnp, tk=D)
    gcol = _mlstm_gates(proj, p["m_b_if"], D)
    grow = gcol[:, :3 * HEADS].reshape(nc, CHUNK, 3 * HEADS).transpose(0, 2, 1)
    carry = gather_plan.get("mlstm_fwd")
    (hg, cs, ns, ms), extra = _mlstm_fwd(proj, gcol, grow, p["m_head_norm"], carry)
    if carry is not None:
        for n, a in zip(carry.names, extra):
            w[n] = carry.post[n](a) if n in carry.post else a
    (h1,) = mm("m_out", _a_plain(hg, "nn", tm, D), _b_plain(w["m_w_out"], "nn", td, D), "nn",
               T, D, D, [(F32, D)], tm=tm, tn=td, tk=D, epi=_epi_resid, extras=[(h0, td, 0)])
    h2, ffn0 = ffn_fwd(0, h1)
    hn2, r2 = _rms_fwd("mix1_norm", h2, vec(p["norm_mix"], 1))
    (pr,) = mm("r_proj", _a_plain(hn2, "nn", tm, D), _b_plain(w["r_w_in"], "nn", td, D), "nn",
               T, 2 * D, D, [(F32, 2 * D)], tm=tm, tn=td, tk=D)
    y, hs = _rglru_fwd(pr, p["conv_w"], p["conv_b"], w["gate_w"], p["gate_b"], p["a_param"])
    (h3,) = mm("r_out", _a_plain(y, "nn", tm, D), _b_plain(w["r_w_out"], "nn", td, D), "nn",
               T, D, D, [(F32, D)], tm=tm, tn=td, tk=D, epi=_epi_resid, extras=[(h2, td, 0)])
    h4, ffn1 = ffn_fwd(1, h3)
    loss, dh, dhb, g["norm_final"] = _loss_head(h4, p["norm_final"], tgt)

    dh, dhb, dnf1 = ffn_bwd(1, h3, ffn1, dh, dhb)
    (dy,) = mm("r_dy", _a_plain(dhb, "nt", tm, D), _b_plain(w["r_w_out"], "nt", td, D), "nt",
               T, D, D, [(F32, D)], tm=tm, tn=td, tk=D)
    (g["r_w_out"],) = mm("r_dwout", _a_plain(y, "tn", td, tkt), _b_plain(dhb, "tn", td, tkt), "tn",
                         D, D, T, [(BF16, D)], tm=td, tn=td, tk=tkt)
    dgate, drec, g["conv_w"], g["conv_b"], dgw, g["gate_b"], g["a_param"] = _rglru_bwd(
        pr, hs, dy, p["conv_w"], p["conv_b"], w["gate_w"], p["gate_b"], p["a_param"])
    g["gate_w"] = dgw.astype(BF16)
    (g["r_w_in"],) = mm("r_dwin", _a_plain(hn2, "tn", td, tkt), _b_split_n(dgate, drec, td, tkt), "tn",
                        D, 2 * D, T, [(BF16, 2 * D)], tm=td, tn=td, tk=tkt)
    (dhn2,) = mm("r_dx", _a_split_k(dgate, drec, tm, td), _b_plain(w["r_w_in"], "nt", td, td), "nt",
                 T, D, 2 * D, [(F32, D)], tm=tm, tn=td, tk=td)
    dh, dhb, dnm1 = _rms_bwd("mix1_dnorm", h2, r2, dhn2, vec(p["norm_mix"], 1), dh)
    dh, dhb, dnf0 = ffn_bwd(0, h1, ffn0, dh, dhb)
    (dhg,) = mm("m_dhg", _a_plain(dhb, "nt", tm, D), _b_plain(w["m_w_out"], "nt", td, D), "nt",
                T, D, D, [(F32, D)], tm=tm, tn=td, tk=D)
    (g["m_w_out"],) = mm("m_dwout", _a_plain(hg, "tn", td, tkt), _b_plain(dhb, "tn", td, tkt), "tn",
                         D, D, T, [(BF16, D)], tm=td, tn=td, tk=tkt)
    dproj, g["m_head_norm"], g["m_b_if"] = _mlstm_bwd(proj, gcol, grow, p["m_head_norm"], dhg, cs, ns, ms)
    (g["m_w_in"],) = mm("m_dwin", _a_plain(hn0, "tn", td, tkt), _b_plain(dproj, "tn", tnp, tkt), "tn",
                        D, pw, T, [(BF16, pw)], tm=td, tn=tnp, tk=tkt)
    (dhn0,) = mm("m_dx", _a_plain(dproj, "nt", tm, tnp), _b_plain(w["m_w_in"], "nt", td, tnp), "nt",
                 T, D, pw, [(F32, D)], tm=tm, tn=td, tk=tnp)
    grad_x, _, dnm0 = _rms_bwd("mix0_dnorm", h0, r0, dhn0, vec(p["norm_mix"], 0), dh)
    g["norm_mix"] = jnp.concatenate([dnm0, dnm1], axis=0)
    g["norm_ffn"] = jnp.concatenate([dnf0, dnf1], axis=0)
    return loss, grad_x, g, landed


_MESH = pl.DeviceIdType.MESH
_HBM = pl.BlockSpec(memory_space=pltpu.HBM)
N_CHIPS = 4


def _place():
    x, y, c = lax.axis_index("x"), lax.axis_index("y"), lax.axis_index("c")
    chips = [(1 - x, y), (x, 1 - y), (1 - x, 1 - y)]
    return x, y, c, 2 * x + y, chips, [2 * px + py for px, py in chips]


class _Piece:
    def __init__(self, kind, rows, cols):
        self.kind, self.rows, self.cols, self.hr = kind, rows, cols, rows // 2

    def full_shape(self):
        return {"col": (self.rows, N_CHIPS * self.cols), "row": (N_CHIPS * self.rows, self.cols),
                "stack": (N_CHIPS, self.rows, self.cols)}[self.kind]

    def region(self, ref, s, h):
        rows = pl.ds(h * self.hr, self.hr)
        if self.kind == "col":
            return ref.at[rows, pl.ds(pl.multiple_of(s * self.cols, LANES), self.cols)]
        if self.kind == "row":
            return ref.at[pl.ds(pl.multiple_of(s * self.rows + h * self.hr, 16), self.hr), :]
        return ref.at[s, rows, :]

    def shard_half(self, ref, h):
        return ref.at[pl.ds(h * self.hr, self.hr), :]


class _Gather:
    def __init__(self, names, shards, pieces, post=None):
        n = len(names)
        self.names, self.ins, self.pieces, self.post = list(names), list(shards), list(pieces), post or {}
        self.out_shapes = [jax.ShapeDtypeStruct(p.full_shape(), s.dtype) for p, s in zip(pieces, shards)]
        self.scratch = [pltpu.SemaphoreType.DMA((n, 2))] + [pltpu.SemaphoreType.DMA((n, 3))] * 4

    def _copies(self, ins, outs, sems):
        loc, snd, rcv, fsnd, frcv = sems
        x, y, c, j, chips, cj = _place()

        def local(w, h):
            return pltpu.make_async_copy(self.pieces[w].shard_half(ins[w], h), self.pieces[w].region(outs[w], j, h),
                                         loc.at[w, h])

        def ici_send(w, k):
            return pltpu.make_async_remote_copy(
                src_ref=self.pieces[w].shard_half(ins[w], c), dst_ref=self.pieces[w].region(outs[w], j, c),
                send_sem=snd.at[w, k], recv_sem=rcv.at[w, k], device_id=(*chips[k], c), device_id_type=_MESH)

        def ici_recv(w, k):
            region = self.pieces[w].region(outs[w], cj[k], c)
            return pltpu.make_async_remote_copy(
                src_ref=region, dst_ref=region, send_sem=snd.at[w, k], recv_sem=rcv.at[w, k],
                device_id=(*chips[k], c), device_id_type=_MESH)

        def d2d(w, k, h):
            region = self.pieces[w].region(outs[w], cj[k], h)
            return pltpu.make_async_remote_copy(
                src_ref=region, dst_ref=region, send_sem=fsnd.at[w, k], recv_sem=frcv.at[w, k],
                device_id=(x, y, 1 - c), device_id_type=_MESH)

        return c, local, ici_send, ici_recv, d2d

    def start(self, ins, outs, sems):
        _, local, ici_send, _, _ = self._copies(ins, outs, sems)
        for w in range(len(self.names)):
            for k in range(3):
                ici_send(w, k).start()
        for w in range(len(self.names)):
            for h in range(2):
                local(w, h).start()

    def finish(self, ins, outs, sems):
        c, local, ici_send, ici_recv, d2d = self._copies(ins, outs, sems)
        n = len(self.names)
        for w in range(n):
            for k in range(3):
                ici_recv(w, k).wait_recv()
                d2d(w, k, c).start()
        for w in range(n):
            for k in range(3):
                d2d(w, k, 1 - c).wait_recv()
        for w in range(n):
            for k in range(3):
                ici_send(w, k).wait_send()
                d2d(w, k, c).wait_send()
            for h in range(2):
                local(w, h).wait()


class _Scatter:
    def __init__(self, names, grads, pieces):
        n = len(names)
        self.names, self.ins, self.pieces = list(names), list(grads), list(pieces)
        self.out_shapes = [jax.ShapeDtypeStruct((7, p.hr, p.cols), g.dtype) for p, g in zip(pieces, grads)]
        self.scratch = [pltpu.SemaphoreType.DMA((n, 7))] * 2

    def _copy(self, ins, outs, sems, w, k):
        x, y, c = lax.axis_index("x"), lax.axis_index("y"), lax.axis_index("c")
        px, py, pc = ((1 - x) if k & 4 else x, (1 - y) if k & 2 else y, (1 - c) if k & 1 else c)
        return pltpu.make_async_remote_copy(
            src_ref=self.pieces[w].region(ins[w], 2 * px + py, pc), dst_ref=outs[w].at[k - 1],
            send_sem=sems[0].at[w, k - 1], recv_sem=sems[1].at[w, k - 1], device_id=(px, py, pc),
            device_id_type=_MESH)

    def start(self, ins, outs, sems):
        for w in range(len(self.names)):
            for k in range(1, 8):
                self._copy(ins, outs, sems, w, k).start()

    def finish(self, ins, outs, sems):
        for w in range(len(self.names)):
            for k in range(1, 8):
                self._copy(ins, outs, sems, w, k).wait()


def _run_exchange(name, ex):
    ni, no = len(ex.ins), len(ex.out_shapes)

    def body(*refs):
        ins, outs, sems = refs[:ni], refs[ni:ni + no], refs[ni + no:]
        ex.start(ins, outs, sems)
        ex.finish(ins, outs, sems)

    return pl.pallas_call(
        body, name=name, in_specs=[_HBM] * ni, out_specs=[_HBM] * no, out_shape=list(ex.out_shapes),
        scratch_shapes=list(ex.scratch),
    )(*ex.ins)


def _rs_sum(name, piece, grad, landed, jc_idx):
    hr, C = piece.hr, piece.cols
    br = _pick(hr, max(16, (1 << 18) // C), 16)
    nb = hr // br
    if piece.kind == "col":
        g_spec = pl.BlockSpec((br, C), lambda i, jc: (jc[1] * nb + i, jc[0]))
    elif piece.kind == "row":
        g_spec = pl.BlockSpec((br, C), lambda i, jc: ((jc[0] * 2 + jc[1]) * nb + i, 0))
    else:
        g_spec = pl.BlockSpec((None, br, C), lambda i, jc: (jc[0], jc[1] * nb + i, 0))

    def body(jc_ref, g_ref, l_ref, o_ref):
        acc = g_ref[...].astype(F32)
        for k in range(7):
            acc = acc + l_ref[k].astype(F32)
        o_ref[0] = acc

    return pl.pallas_call(
        body, name=name,
        grid_spec=pltpu.PrefetchScalarGridSpec(
            num_scalar_prefetch=1, grid=(nb,),
            in_specs=[g_spec, pl.BlockSpec((7, br, C), lambda i, jc: (0, i, 0))],
            out_specs=pl.BlockSpec((1, br, C), lambda i, jc: (jc[1], i, 0))),
        out_shape=jax.ShapeDtypeStruct((2, hr, C), F32),
        compiler_params=_cparams(("parallel",)),
    )(jc_idx, grad, landed)


def _rs_share_halves(rs):
    n = len(rs)

    def body(*refs):
        outs = refs[n:2 * n]
        send_sems, recv_sems = refs[2 * n:]
        x, y, c, _, _, _ = _place()
        cps = []
        for w in range(n):
            cps.append(pltpu.make_async_remote_copy(
                src_ref=outs[w].at[c], dst_ref=outs[w].at[c], send_sem=send_sems.at[w],
                recv_sem=recv_sems.at[w], device_id=(x, y, 1 - c), device_id_type=_MESH))
            cps[-1].start()
        for w in range(n):
            cps[w].wait_send()
            pltpu.make_async_remote_copy(
                src_ref=outs[w].at[1 - c], dst_ref=outs[w].at[1 - c], send_sem=send_sems.at[w],
                recv_sem=recv_sems.at[w], device_id=(x, y, 1 - c), device_id_type=_MESH).wait_recv()

    return pl.pallas_call(
        body, name="rs_share_halves",
        in_specs=[_HBM] * n, out_specs=[_HBM] * n,
        out_shape=[jax.ShapeDtypeStruct(a.shape, a.dtype) for a in rs],
        input_output_aliases={w: w for w in range(n)},
        scratch_shapes=[pltpu.SemaphoreType.DMA((n,))] * 2,
    )(*rs)


def _allreduce_small(v):
    R = v.shape[0]

    def body(v_ref, o_ref, slots, send_sems, recv_sems):
        x, y, c = lax.axis_index("x"), lax.axis_index("y"), lax.axis_index("c")
        me = 4 * x + 2 * y + c
        cps = []
        for k in range(1, 8):
            peer = ((1 - x) if k & 4 else x, (1 - y) if k & 2 else y, (1 - c) if k & 1 else c)
            cps.append(pltpu.make_async_remote_copy(
                src_ref=v_ref, dst_ref=slots.at[k - 1], send_sem=send_sems.at[k - 1],
                recv_sem=recv_sems.at[k - 1], device_id=peer, device_id_type=_MESH))
            cps[-1].start()
        for cp in cps:
            cp.wait()
        acc = jnp.zeros((R, LANES), F32)
        for d in range(8):
            k = ((d // 4) ^ x) * 4 + (((d // 2) % 2) ^ y) * 2 + ((d % 2) ^ c)
            other = slots[jnp.maximum(k - 1, 0)]
            acc = acc + jnp.where(d == me, v_ref[...], other)
        o_ref[...] = acc

    vm = pl.BlockSpec(memory_space=pltpu.VMEM)
    return pl.pallas_call(
        body, name="allreduce_small",
        in_specs=[vm], out_specs=vm,
        out_shape=jax.ShapeDtypeStruct((R, LANES), F32),
        scratch_shapes=[pltpu.VMEM((7, R, LANES), F32), pltpu.SemaphoreType.DMA((7,)), pltpu.SemaphoreType.DMA((7,))],
    )(v)


def _rows(a):
    flat = a.reshape(-1)
    return jnp.pad(flat, (0, (-flat.shape[0]) % LANES)).reshape(-1, LANES)


def _pack_rows(parts, mult=8):
    v = jnp.concatenate([_rows(a) for a in parts], axis=0)
    return jnp.pad(v, ((0, (-v.shape[0]) % mult), (0, 0)))


def _unpack_rows(v, shapes):
    out, r = [], 0
    for s in shapes:
        size = math.prod(s)
        nr = -(-size // LANES)
        out.append(v[r:r + nr].reshape(-1)[:size].reshape(s))
        r += nr
    return out


def kernel(x, norm_mix, norm_ffn, norm_final, m_w_in, m_b_if, m_head_norm, m_w_out, r_w_in, r_conv_w, r_conv_b, r_gate_w, r_gate_b, r_a_param, r_w_out, ffn_w_in, ffn_w_out, loss_target, m_norm_mix, m_norm_ffn, m_norm_final, m_m_w_in, m_m_b_if, m_m_head_norm, m_m_w_out, m_r_w_in, m_r_conv_w, m_r_conv_b, m_r_gate_w, m_r_gate_b, m_r_a_param, m_r_w_out, m_ffn_w_in, m_ffn_w_out, v_norm_mix, v_norm_ffn, v_norm_final, v_m_w_in, v_m_b_if, v_m_head_norm, v_m_w_out, v_r_w_in, v_r_conv_w, v_r_conv_b, v_r_gate_w, v_r_gate_b, v_r_a_param, v_r_w_out, v_ffn_w_in, v_ffn_w_out):
    names = ["norm_mix", "norm_ffn", "norm_final", "m_w_in", "m_b_if", "m_head_norm", "m_w_out", "r_w_in", "r_conv_w",
             "r_conv_b", "r_gate_w", "r_gate_b", "r_a_param", "r_w_out", "ffn_w_in", "ffn_w_out"]
    w = dict(zip(names, [norm_mix, norm_ffn, norm_final, m_w_in, m_b_if, m_head_norm, m_w_out, r_w_in, r_conv_w,
                         r_conv_b, r_gate_w, r_gate_b, r_a_param, r_w_out, ffn_w_in, ffn_w_out]))
    mom = dict(zip(names, [m_norm_mix, m_norm_ffn, m_norm_final, m_m_w_in, m_m_b_if, m_m_head_norm, m_m_w_out, m_r_w_in,
                           m_r_conv_w, m_r_conv_b, m_r_gate_w, m_r_gate_b, m_r_a_param, m_r_w_out, m_ffn_w_in, m_ffn_w_out]))
    var = dict(zip(names, [v_norm_mix, v_norm_ffn, v_norm_final, v_m_w_in, v_m_b_if, v_m_head_norm, v_m_w_out, v_r_w_in,
                           v_r_conv_w, v_r_conv_b, v_r_gate_w, v_r_gate_b, v_r_a_param, v_r_w_out, v_ffn_w_in, v_ffn_w_out]))
    D = x.shape[-1]
    W = D // RBLOCKS
    nproj = m_w_in.shape[2] * N_CHIPS
    shard_col = lax.axis_index("x") * 2 + lax.axis_index("y")
    jc_idx = jnp.stack([shard_col, lax.axis_index("c")]).astype(jnp.int32)

    def cols(g, lead):
        nl = len(lead)
        g = g.reshape((N_CHIPS,) + lead + (-1,))
        g = jnp.moveaxis(g, 0, nl)
        return g.reshape(lead + (-1,))

    def uncols(a, lead):
        nl = len(lead)
        a = a.reshape(lead + (N_CHIPS, -1))
        a = jnp.moveaxis(a, nl, 0)
        return a.reshape(N_CHIPS, math.prod(lead), -1)

    shard = {"m_w_in": m_w_in[0], "m_w_out": m_w_out[0], "r_w_in": r_w_in[0],
             "gate_w": r_gate_w[0].reshape(-1, r_gate_w.shape[-1]), "r_w_out": r_w_out[0]}
    kind = {"m_w_in": "stack", "m_w_out": "row", "r_w_in": "col", "gate_w": "stack", "r_w_out": "row"}
    for l in range(2):
        shard[f"ffn{l}_w_in"], kind[f"ffn{l}_w_in"] = ffn_w_in[l], "col"
        shard[f"ffn{l}_w_out"], kind[f"ffn{l}_w_out"] = ffn_w_out[l], "row"
    shard = {n: a.astype(BF16) for n, a in shard.items()}
    piece = {n: _Piece(kind[n], *shard[n].shape) for n in shard}
    small_sharded = ["r_conv_w", "r_conv_b", "r_gate_b", "r_a_param"]
    small_pack = _pack_rows([w[n] for n in small_sharded], mult=16)
    post = {"m_w_in": lambda a: jnp.pad(cols(a, (D,)), ((0, 0), (0, 3 * D + IF_PAD - nproj))),
            "gate_w": lambda a: cols(a, (RBLOCKS, W))}

    def gather(names):
        return _Gather(names, [shard[n] for n in names], [piece[n] for n in names], post)

    first = _Gather(["m_w_in", "small"], [shard["m_w_in"], small_pack],
                    [piece["m_w_in"], _Piece("stack", *small_pack.shape)], post)
    got = dict(zip(first.names, _run_exchange("gather_first", first)))
    weights = {"m_w_in": post["m_w_in"](got["m_w_in"])}
    gather_plan = {"m_proj": gather(["m_w_out", "ffn0_w_out"]),
                   "mlstm_fwd": gather(["ffn0_w_in", "r_w_in", "gate_w", "r_w_out"]),
                   "ffn0_in": gather(["ffn1_w_in"]), "ffn0_out": gather(["ffn1_w_out"])}
    sp = got["small"]
    ws = D // N_CHIPS
    wg = 2 * W // N_CHIPS
    r1, r2, r3 = CONV_W * ws // LANES, (CONV_W + 1) * ws // LANES, ((CONV_W + 1) * ws + RBLOCKS * wg) // LANES
    conv_w_f = sp[:, :r1].reshape(N_CHIPS, CONV_W, ws).transpose(1, 0, 2).reshape(CONV_W, D)
    conv_b_f = sp[:, r1:r2].reshape(1, D)
    gate_b_f = sp[:, r2:r3].reshape(N_CHIPS, RBLOCKS, wg).transpose(1, 0, 2).reshape(RBLOCKS, 1, 2 * W)
    a_param_f = sp[:, r3:r3 + ws // LANES].reshape(1, D)
    p = dict(
        norm_mix=norm_mix, norm_ffn=norm_ffn, norm_final=norm_final.reshape(1, D),
        m_b_if=jnp.pad(m_b_if, ((0, 0), (0, IF_PAD - m_b_if.shape[1]))), m_head_norm=m_head_norm,
        conv_w=conv_w_f, conv_b=conv_b_f, gate_b=gate_b_f, a_param=a_param_f,
    )

    gsrc = {}

    def scatter(names):
        def make(g):
            for n in names:
                if n == "m_w_in":
                    gsrc[n] = uncols(g[n][:, :nproj], (D,))
                elif n == "gate_w":
                    gsrc[n] = uncols(g[n], (RBLOCKS, W))
                else:
                    gsrc[n] = g[n]
            return _Scatter(names, [gsrc[n] for n in names], [piece[n] for n in names])
        return make

    scatter_plan = {"ffn1_dwin": scatter(["ffn1_w_out"]), "ffn1_dx": scatter(["ffn1_w_in"]),
                    "r_dx": scatter(["r_w_out", "gate_w"]), "ffn0_dact": scatter(["r_w_in"]),
                    "ffn0_dwin": scatter(["ffn0_w_out"]), "ffn0_dx": scatter(["ffn0_w_in"]),
                    "m_dwin": scatter(["m_w_out"]), "m_dx": scatter(["m_w_in"])}
    loss_part, grad_x, g, landed = _step(x[0], loss_target[0], p, weights, gather_plan, scatter_plan)

    big = list(shard)
    halves = [_rs_sum(f"rs_sum_{n}", piece[n], gsrc[n], landed[n], jc_idx) for n in big]
    reduced = {n: r.reshape(shard[n].shape) for n, r in zip(big, _rs_share_halves(halves))}

    small_all = ["norm_mix", "norm_ffn", "norm_final", "m_b_if", "m_head_norm", "r_conv_w", "r_conv_b", "r_gate_b", "r_a_param"]
    gsmall = [g["norm_mix"], g["norm_ffn"], g["norm_final"], g["m_b_if"], g["m_head_norm"], g["conv_w"], g["conv_b"],
              g["gate_b"].reshape(RBLOCKS, 2 * W), g["a_param"], loss_part]
    summed = _unpack_rows(_allreduce_small(_pack_rows(gsmall)), [a.shape for a in gsmall])
    loss = summed[-1][0, 0]
    gs = dict(zip(small_all, summed[:-1]))
    gs["norm_final"] = gs["norm_final"].reshape(D)
    gs["m_b_if"] = gs["m_b_if"][:, :m_b_if.shape[1]]
    gs["r_conv_w"] = lax.dynamic_slice_in_dim(gs["r_conv_w"], shard_col * ws, ws, axis=1).reshape(r_conv_w.shape)
    gs["r_conv_b"] = lax.dynamic_slice_in_dim(gs["r_conv_b"], shard_col * ws, ws, axis=1).reshape(r_conv_b.shape)
    gs["r_gate_b"] = lax.dynamic_slice_in_dim(gs["r_gate_b"], shard_col * (2 * W // N_CHIPS), 2 * W // N_CHIPS,
                                              axis=1).reshape(r_gate_b.shape)
    gs["r_a_param"] = lax.dynamic_slice_in_dim(gs["r_a_param"], shard_col * ws, ws, axis=1).reshape(r_a_param.shape)

    grads, delta, new_m, new_v = {}, {}, {}, {}
    layers = {"m_w_in": ["m_w_in"], "m_w_out": ["m_w_out"], "r_w_in": ["r_w_in"], "r_gate_w": ["gate_w"],
              "r_w_out": ["r_w_out"], "ffn_w_in": ["ffn0_w_in", "ffn1_w_in"], "ffn_w_out": ["ffn0_w_out", "ffn1_w_out"]}
    for n, parts in layers.items():
        shp = w[n].shape
        if n == "m_w_in":
            tr_ = lambda a: jnp.swapaxes(a, -1, -2)
            res = _adamw(f"adamw_{n}", tr_(w[n]), [tr_(reduced[n])], tr_(mom[n]), tr_(var[n]))
            grads[n], delta[n], new_m[n], new_v[n] = (tr_(a) for a in res)
            continue
        as3d = lambda a: a.reshape((len(parts),) + shard[parts[0]].shape)
        res = _adamw(f"adamw_{n}", as3d(w[n]), [reduced[q] for q in parts], as3d(mom[n]), as3d(var[n]))
        grads[n], delta[n], new_m[n], new_v[n] = (a.reshape(shp) for a in res)

    packs = [_pack_rows([t[n] for n in small_all])[None] for t in (w, gs, mom, var)]
    outs = _adamw("adamw_small", packs[0], [packs[1][0]], packs[2], packs[3])
    for dst, pk in zip((delta, new_m, new_v), outs[1:]):
        for n, a in zip(small_all, _unpack_rows(pk[0], [w[n].shape for n in small_all])):
            dst[n] = a
    for n in small_all:
        grads[n] = gs[n]
    return (loss, grad_x[None], *[grads[n] for n in names], *[delta[n] for n in names],
            *[new_m[n] for n in names], *[new_v[n] for n in names])
```

```python
import functools
import math

import jax
import jax.numpy as jnp
from jax import lax
from jax.experimental import pallas as pl
from jax.experimental.pallas import tpu as pltpu

F32 = jnp.float32
BF16 = jnp.bfloat16

EPS = 1e-6
HEADS = 8
CHUNK = 256
RBLOCKS = 8
CONV_W = 4
R_C = 8.0
LANES = 128
IF_PAD = LANES
VMEM_LIMIT = 52 << 20

ADAM_LR = 0.001
ADAM_B1 = 0.9
ADAM_B2 = 0.999
ADAM_EPS = 1e-08
ADAM_WD = 0.01
ADAM_STEP = 10

_DIMS = {
    "nn": (((1,), (0,)), ((), ())),
    "nt": (((1,), (1,)), ((), ())),
    "tn": (((0,), (0,)), ((), ())),
}


def _dot(a, b, mode="nn"):
    return lax.dot_general(a, b, _DIMS[mode], preferred_element_type=F32)


def _pick(n, pref, mult):
    best = None
    d = mult
    while d <= min(n, pref):
        if n % d == 0:
            best = d
        d += mult
    return best if best is not None else n


def _cparams(sem):
    return pltpu.CompilerParams(dimension_semantics=sem, vmem_limit_bytes=VMEM_LIMIT)


def _a_plain(a, mode, tm, tk):
    if mode == "tn":
        return [(a, (tk, tm), lambda i, j, k: (k, i))], lambda t, i, j, k: t[0]
    return [(a, (tm, tk), lambda i, j, k: (i, k))], lambda t, i, j, k: t[0]


def _a_split_k(a1, a2, tm, tk):
    h = a1.shape[1] // tk
    return ([(a1, (tm, tk), lambda i, j, k: (i, jnp.minimum(k, h - 1))),
             (a2, (tm, tk), lambda i, j, k: (i, jnp.maximum(k - h, 0)))],
            lambda t, i, j, k: jnp.where(k < h, t[0], t[1]))


def _sigmoid(x):
    return 1.0 / (1.0 + jnp.exp(-x))


def _epi_swiglu(acc):
    half = acc.shape[1] // 2
    g, u = acc[:, :half], acc[:, half:]
    return g, u, g * _sigmoid(g) * u


def _b_plain(b, mode, tn, tk):
    if mode == "nt":
        return [(b, (tn, tk), lambda i, j, k: (j, k))], lambda t, i, j, k: t[0]
    return [(b, (tk, tn), lambda i, j, k: (k, j))], lambda t, i, j, k: t[0]


def _b_split_n(b1, b2, tn, tk):
    h = b1.shape[1] // tn
    return ([(b1, (tk, tn), lambda i, j, k: (jnp.where(j < h, k, 0), jnp.minimum(j, h - 1))),
             (b2, (tk, tn), lambda i, j, k: (jnp.where(j < h, 0, k), jnp.maximum(j - h, 0)))],
            lambda t, i, j, k: jnp.where(j < h, t[0], t[1]))


def _b_pair(b, tn, tk):
    off = b.shape[1] // 2 // tn
    return ([(b, (tk, tn), lambda i, j, k: (k, j)), (b, (tk, tn), lambda i, j, k: (k, j + off))],
            lambda t, i, j, k: jnp.concatenate(t, axis=1))


def _mm(name, a_op, b_op, mode, M, N, K, out_defs, *, tm, tn, tk, epi=None, extras=(), carry=None):
    assert M % tm == 0 and N % tn == 0 and K % tk == 0, (name, M, N, K, tm, tn, tk)
    (a_in, a_fn), (b_in, b_fn) = a_op, b_op
    ni, nj, nk = M // tm, N // tn, K // tk
    na, nb, ne, no = len(a_in), len(b_in), len(extras), len(out_defs)
    nci = len(carry.ins) if carry else 0
    nco = len(carry.out_shapes) if carry else 0
    ncs = len(carry.scratch) if carry else 0

    def body(*refs):
        pos = 0

        def take(n):
            nonlocal pos
            r = refs[pos:pos + n]
            pos += n
            return r

        a_refs, b_refs, ex, c_in, outs, c_out = take(na), take(nb), take(ne), take(nci), take(no), take(nco)
        acc_ref = take(1)[0] if nk > 1 else None
        c_scr = take(ncs)
        i, j, k = pl.program_id(0), pl.program_id(1), pl.program_id(2)
        if carry:
            @pl.when((i == 0) & (j == 0) & (k == 0))
            def _():
                carry.start(c_in, c_out, c_scr)

        def finish(acc):
            res = epi(acc, *[e[...] for e in ex]) if epi is not None else (acc,)
            for o_ref, r in zip(outs, res):
                o_ref[...] = r.astype(o_ref.dtype)

        av = a_fn([r[...] for r in a_refs], i, j, k).astype(BF16)
        bv = b_fn([r[...] for r in b_refs], i, j, k).astype(BF16)
        d = _dot(av, bv, mode)
        if nk == 1:
            finish(d)
        else:
            @pl.when(k == 0)
            def _():
                acc_ref[...] = d

            @pl.when(k > 0)
            def _():
                acc_ref[...] += d

            @pl.when(k == nk - 1)
            def _():
                finish(acc_ref[...])
        if carry:
            @pl.when((i == ni - 1) & (j == nj - 1) & (k == nk - 1))
            def _():
                carry.finish(c_in, c_out, c_scr)

    def spec(shape, imap):
        return pl.BlockSpec(shape, imap)

    in_specs = [spec(s, m) for _, s, m in a_in + b_in]
    in_specs += [pl.BlockSpec((tm, w), lambda i, j, k, off=off: (i, j + off)) for _, w, off in extras]
    out_specs = [pl.BlockSpec((tm, c * tn // N), lambda i, j, k: (i, j)) for _, c in out_defs]
    out_shape = [jax.ShapeDtypeStruct((M, c), dt) for dt, c in out_defs]
    scratch = [pltpu.VMEM((tm, tn), F32)] if nk > 1 else []
    operands = [x[0] for x in a_in + b_in] + [e[0] for e in extras]
    if carry:
        in_specs += [_HBM] * nci
        out_specs += [_HBM] * nco
        out_shape += list(carry.out_shapes)
        scratch += list(carry.scratch)
        operands += list(carry.ins)
    sem = ("arbitrary",) * 3 if carry else ("parallel", "parallel", "arbitrary")
    res = pl.pallas_call(
        body, name=name, grid=(ni, nj, nk), in_specs=in_specs, out_specs=out_specs, out_shape=out_shape,
        scratch_shapes=scratch, compiler_params=_cparams(sem),
    )(*operands)
    return res[:no], res[no:]


def _epi_resid(acc, resid):
    return (resid + acc,)


def _epi_swiglu_bwd(da, g, u):
    g = g.astype(F32)
    u = u.astype(F32)
    sig = _sigmoid(g)
    return da * u * (sig * (1.0 + g * (1.0 - sig))), da * (g * sig)


def _rms_fwd(name, x, g):
    T, D = x.shape
    tr = _pick(T, 512, 16)

    def body(x_ref, g_ref, hn_ref, r_ref):
        xv = x_ref[...]
        r = lax.rsqrt(jnp.mean(xv * xv, axis=-1, keepdims=True) + EPS)
        hn_ref[...] = (xv * r * g_ref[...]).astype(BF16)
        r_ref[...] = r

    return pl.pallas_call(
        body, name=name, grid=(T // tr,),
        in_specs=[pl.BlockSpec((tr, D), lambda i: (i, 0)), pl.BlockSpec((1, D), lambda i: (0, 0))],
        out_specs=[pl.BlockSpec((tr, D), lambda i: (i, 0)), pl.BlockSpec((tr, 1), lambda i: (i, 0))],
        out_shape=[jax.ShapeDtypeStruct((T, D), BF16), jax.ShapeDtypeStruct((T, 1), F32)],
        compiler_params=_cparams(("parallel",)),
    )(x, g)


def _rms_bwd(name, x, r, dy, g, dres):
    T, D = x.shape
    tr = _pick(T, 512, 16)

    def body(x_ref, r_ref, dy_ref, g_ref, dres_ref, dh_ref, dhb_ref, dg_ref):
        xh = x_ref[...] * r_ref[...]
        dy_v = dy_ref[...]
        dyg = dy_v * g_ref[...]
        m = jnp.mean(dyg * xh, axis=-1, keepdims=True)
        dh = dres_ref[...] + r_ref[...] * (dyg - xh * m)
        dh_ref[...] = dh
        dhb_ref[...] = dh.astype(BF16)

        @pl.when(pl.program_id(0) == 0)
        def _():
            dg_ref[...] = jnp.zeros_like(dg_ref)

        dg_ref[...] += jnp.sum(dy_v * xh, axis=0, keepdims=True)

    row = pl.BlockSpec((tr, D), lambda i: (i, 0))
    vec = pl.BlockSpec((1, D), lambda i: (0, 0))
    return pl.pallas_call(
        body, name=name, grid=(T // tr,),
        in_specs=[row, pl.BlockSpec((tr, 1), lambda i: (i, 0)), row, vec, row],
        out_specs=[row, row, vec],
        out_shape=[jax.ShapeDtypeStruct((T, D), F32), jax.ShapeDtypeStruct((T, D), BF16),
                   jax.ShapeDtypeStruct((1, D), F32)],
        compiler_params=_cparams(("arbitrary",)),
    )(x, r, dy, g, dres)


def _loss_head(x, g, tgt):
    T, D = x.shape
    tr = _pick(T, 512, 16)

    def body(x_ref, g_ref, t_ref, loss_ref, dh_ref, dhb_ref, dg_ref):
        xv = x_ref[...]
        gv = g_ref[...]
        r = lax.rsqrt(jnp.mean(xv * xv, axis=-1, keepdims=True) + EPS)
        xh = xv * r
        e = xh * gv - t_ref[...]
        part = 0.5 * jnp.sum(jnp.mean(e * e, axis=-1, keepdims=True), axis=0, keepdims=True)
        dy = e * (1.0 / D)
        dyg = dy * gv
        m = jnp.mean(dyg * xh, axis=-1, keepdims=True)
        dh = r * (dyg - xh * m)
        dh_ref[...] = dh
        dhb_ref[...] = dh.astype(BF16)

        @pl.when(pl.program_id(0) == 0)
        def _():
            dg_ref[...] = jnp.zeros_like(dg_ref)
            loss_ref[...] = jnp.zeros_like(loss_ref)

        dg_ref[...] += jnp.sum(dy * xh, axis=0, keepdims=True)
        loss_ref[...] += jnp.broadcast_to(part, loss_ref.shape)

    row = pl.BlockSpec((tr, D), lambda i: (i, 0))
    vec = pl.BlockSpec((1, D), lambda i: (0, 0))
    return pl.pallas_call(
        body, name="loss_head", grid=(T // tr,),
        in_specs=[row, vec, row],
        out_specs=[pl.BlockSpec((1, LANES), lambda i: (0, 0)), row, row, vec],
        out_shape=[jax.ShapeDtypeStruct((1, LANES), F32), jax.ShapeDtypeStruct((T, D), F32),
                   jax.ShapeDtypeStruct((T, D), BF16), jax.ShapeDtypeStruct((1, D), F32)],
        compiler_params=_cparams(("arbitrary",)),
    )(x, g, tgt)


def _mlstm_gates(proj, b_if_pad, D):
    T = proj.shape[0]
    tr = _pick(T, 512, CHUNK)

    def body(p_ref, b_ref, o_ref):
        z = p_ref[...] + b_ref[...]
        lane = lax.broadcasted_iota(jnp.int32, z.shape, 1)
        row = lax.broadcasted_iota(jnp.int32, z.shape, 0) % CHUNK
        lf = jnp.where((lane >= HEADS) & (lane < 2 * HEADS),
                       jnp.minimum(z, 0.0) - jnp.log(1.0 + jnp.exp(-jnp.abs(z))), 0.0)
        c = lf
        d = 1
        while d < CHUNK:
            c = c + jnp.where(row >= d, pltpu.roll(c, d, 0), 0.0)
            d *= 2
        o_ref[...] = jnp.where(lane < HEADS, z, 0.0) + lf + pltpu.roll(c, HEADS, 1)

    return pl.pallas_call(
        body, name="mlstm_gates", grid=(T // tr,),
        in_specs=[pl.BlockSpec((tr, IF_PAD), lambda i: (i, 3 * D // IF_PAD)),
                  pl.BlockSpec((1, IF_PAD), lambda i: (0, 0))],
        out_specs=pl.BlockSpec((tr, IF_PAD), lambda i: (i, 0)),
        out_shape=jax.ShapeDtypeStruct((T, IF_PAD), F32),
        compiler_params=_cparams(("parallel",)),
    )(proj, b_if_pad)


def _mlstm_chunk_common(qf, kf, vf, gc, gr, h, c_prev, n_prev, m_prev, causal):
    L = CHUNK
    qb = qf.astype(BF16)
    kb = kf.astype(BF16)
    vb = vf.astype(BF16)
    i_col = gc[:, h:h + 1]
    b_col = gc[:, 2 * HEADS + h:2 * HEADS + h + 1]
    i_row = gr[h:h + 1, :]
    b_row = gr[2 * HEADS + h:2 * HEADS + h + 1, :]
    dmat = jnp.where(causal, b_col - b_row + i_row, -jnp.inf)
    m_inter = b_col + m_prev
    m_t = jnp.maximum(m_inter, jnp.max(dmat, axis=-1, keepdims=True))
    w = jnp.exp(dmat - m_t)
    qk = _dot(qb, kb, "nt")
    s = qk * w
    sc = jnp.exp(m_inter - m_t)
    cb = c_prev.astype(BF16)
    num_inter = _dot(qb, cb)
    nb = n_prev.astype(BF16).astype(F32)
    qn = jnp.sum(qb.astype(F32) * nb, axis=-1, keepdims=True)
    num = _dot(s.astype(BF16), vb) + sc * num_inter
    den = jnp.sum(s, axis=-1, keepdims=True) + sc * qn
    e_m = jnp.exp(-m_t)
    nrm = jnp.maximum(jnp.abs(den), e_m)
    hh = num / nrm
    b_last = b_row[:, L - 1:L]
    g_row = b_last - b_row + i_row
    g_col = b_last - b_col + i_col
    m_new = jnp.maximum(b_last + m_prev, jnp.max(g_row, axis=-1, keepdims=True))
    wk = jnp.exp(g_col - m_new)
    decay = jnp.exp(b_last + m_prev - m_new)
    kw = kf * wk
    return dict(qb=qb, kb=kb, vb=vb, w=w, s=s, sc=sc, cb=cb, nb=nb, qn=qn, num_inter=num_inter,
                den=den, e_m=e_m, nrm=nrm, hh=hh, m_new=m_new, wk=wk, decay=decay, kw=kw)


def _mlstm_specs(T, D):
    qk = D // 2
    row = lambda w, j: pl.BlockSpec((CHUNK, w), lambda c, w=w, j=j: (c, j))
    return [row(qk, 0), row(qk, 1), row(D, 1), row(D, 2)]


def _mlstm_fwd(proj, gcol, grow, head_norm, carry=None):
    T = proj.shape[0]
    D = head_norm.shape[1]
    nc = T // CHUNK
    dk, dv = D // 2 // HEADS, D // HEADS
    kscale = dk ** -0.5
    L = CHUNK
    nci = len(carry.ins) if carry else 0
    nco = len(carry.out_shapes) if carry else 0

    def body(*refs):
        q_ref, k_ref, v_ref, o_ref, gc_ref, gr_ref, hn_ref = refs[:7]
        c_in = refs[7:7 + nci]
        hg_ref, cs_ref, ns_ref, ms_ref = refs[7 + nci:11 + nci]
        c_out = refs[11 + nci:11 + nci + nco]
        c_s, n_s, m_s = refs[11 + nci + nco:14 + nci + nco]
        c_scr = refs[14 + nci + nco:]

        @pl.when(pl.program_id(0) == 0)
        def _():
            c_s[...] = jnp.zeros_like(c_s)
            n_s[...] = jnp.zeros_like(n_s)
            m_s[...] = jnp.zeros_like(m_s)
            if carry:
                carry.start(c_in, c_out, c_scr)

        cs_ref[0] = c_s[...]
        ns_ref[0] = n_s[...]
        ms_ref[0] = m_s[...]
        causal = lax.broadcasted_iota(jnp.int32, (L, L), 1) <= lax.broadcasted_iota(jnp.int32, (L, L), 0)
        gc = gc_ref[...]
        gr = gr_ref[0]
        for h in range(HEADS):
            qf = q_ref[:, h * dk:(h + 1) * dk]
            kf = k_ref[:, h * dk:(h + 1) * dk] * kscale
            vf = v_ref[:, h * dv:(h + 1) * dv]
            m_prev = m_s[h:h + 1, 0:1]
            f = _mlstm_chunk_common(qf, kf, vf, gc, gr, h, c_s[h], n_s[h:h + 1, :], m_prev, causal)
            c_s[h] = f["decay"] * c_s[h] + _dot(f["kw"].astype(BF16), f["vb"], "tn")
            n_s[h:h + 1, :] = f["decay"] * n_s[h:h + 1, :] + jnp.sum(f["kw"], axis=0, keepdims=True)
            m_s[h:h + 1, :] = jnp.broadcast_to(f["m_new"], (1, LANES))
            hh = f["hh"]
            rs = lax.rsqrt(jnp.mean(hh * hh, axis=-1, keepdims=True) + EPS)
            sl = slice(h * dv, (h + 1) * dv)
            hg_ref[:, sl] = (hh * rs * hn_ref[:, sl] * _sigmoid(o_ref[:, sl])).astype(BF16)
        if carry:
            @pl.when(pl.program_id(0) == nc - 1)
            def _():
                carry.finish(c_in, c_out, c_scr)

    res = pl.pallas_call(
        body, name="mlstm_fwd", grid=(nc,),
        in_specs=_mlstm_specs(T, D) + [
            pl.BlockSpec((L, IF_PAD), lambda c: (c, 0)),
            pl.BlockSpec((1, 3 * HEADS, L), lambda c: (c, 0, 0)),
            pl.BlockSpec((1, D), lambda c: (0, 0)),
        ] + [_HBM] * nci,
        out_specs=[
            pl.BlockSpec((L, D), lambda c: (c, 0)),
            pl.BlockSpec((1, HEADS, dk, dv), lambda c: (c, 0, 0, 0)),
            pl.BlockSpec((1, HEADS, dk), lambda c: (c, 0, 0)),
            pl.BlockSpec((1, HEADS, LANES), lambda c: (c, 0, 0)),
        ] + [_HBM] * nco,
        out_shape=[
            jax.ShapeDtypeStruct((T, D), BF16),
            jax.ShapeDtypeStruct((nc, HEADS, dk, dv), F32),
            jax.ShapeDtypeStruct((nc, HEADS, dk), F32),
            jax.ShapeDtypeStruct((nc, HEADS, LANES), F32),
        ] + (list(carry.out_shapes) if carry else []),
        scratch_shapes=[pltpu.VMEM((HEADS, dk, dv), F32), pltpu.VMEM((HEADS, dk), F32),
                        pltpu.VMEM((HEADS, LANES), F32)] + (list(carry.scratch) if carry else []),
        compiler_params=_cparams(("arbitrary",)),
    )(proj, proj, proj, proj, gcol, grow, head_norm, *(carry.ins if carry else []))
    return res[:4], res[4:]


def _mlstm_bwd(proj, gcol, grow, head_norm, dhg, cs, ns, ms):
    T = proj.shape[0]
    D = head_norm.shape[1]
    nc = T // CHUNK
    dk, dv = D // 2 // HEADS, D // HEADS
    qkw = D // 2
    kscale = dk ** -0.5
    L = CHUNK
    pw = 3 * D + IF_PAD

    def body(q_ref, k_ref, v_ref, o_ref, gc_ref, gr_ref, hn_ref, dhg_ref, cs_ref, ns_ref, ms_ref,
             dp_ref, dgain_ref, dbif_ref, dc_s, dn_s):
        @pl.when(pl.program_id(0) == 0)
        def _():
            dc_s[...] = jnp.zeros_like(dc_s)
            dn_s[...] = jnp.zeros_like(dn_s)
            dgain_ref[...] = jnp.zeros_like(dgain_ref)
            dbif_ref[...] = jnp.zeros_like(dbif_ref)

        rowl = lax.broadcasted_iota(jnp.int32, (L, L), 0)
        coll = lax.broadcasted_iota(jnp.int32, (L, L), 1)
        causal = coll <= rowl
        eye = coll == rowl
        lane = lax.broadcasted_iota(jnp.int32, (L, IF_PAD), 1)
        rowg = lax.broadcasted_iota(jnp.int32, (L, IF_PAD), 0)
        gc = gc_ref[...]
        gr = gr_ref[0]
        dgate = jnp.zeros((L, IF_PAD), F32)
        for h in range(HEADS):
            qf = q_ref[:, h * dk:(h + 1) * dk]
            kf = k_ref[:, h * dk:(h + 1) * dk] * kscale
            vf = v_ref[:, h * dv:(h + 1) * dv]
            c_prev = cs_ref[0, h]
            n_prev = ns_ref[0, h:h + 1, :]
            m_prev = ms_ref[0, h:h + 1, 0:1]
            f = _mlstm_chunk_common(qf, kf, vf, gc, gr, h, c_prev, n_prev, m_prev, causal)
            qb, kb, vb, w, s, sc = f["qb"], f["kb"], f["vb"], f["w"], f["s"], f["sc"]
            hh, nrm, den = f["hh"], f["nrm"], f["den"]
            sl = slice(h * dv, (h + 1) * dv)
            dhg_v = dhg_ref[:, sl]
            sig = _sigmoid(o_ref[:, sl])
            gain = hn_ref[:, sl]
            rs = lax.rsqrt(jnp.mean(hh * hh, axis=-1, keepdims=True) + EPS)
            hn_v = hh * rs
            d_o = dhg_v * hn_v * gain * sig * (1.0 - sig)
            dgain_ref[:, sl] += jnp.sum(dhg_v * hn_v * sig, axis=0, keepdims=True)
            dhn = dhg_v * gain * sig
            dh = rs * (dhn - hn_v * jnp.mean(dhn * hn_v, axis=-1, keepdims=True))
            dnum = dh / nrm
            ddn = -jnp.sum(dh * hh, axis=-1, keepdims=True) / nrm
            dden = jnp.where(jnp.abs(den) > f["e_m"], jnp.where(den > 0.0, ddn, -ddn), 0.0)
            dnum_b = dnum.astype(BF16)
            d_s = _dot(dnum_b, vb, "nt") + dden
            dqk = (d_s * w).astype(BF16)
            p = d_s * s
            dcn = dc_s[h]
            dcb = dcn.astype(BF16)
            dnn = dn_s[h:h + 1, :]
            kwb = f["kw"].astype(BF16)
            d_v = _dot(s.astype(BF16), dnum_b, "tn") + _dot(kwb, dcb)
            dkw = _dot(vb, dcb, "nt") + dnn
            d_q = _dot(dqk, kb) + sc * (_dot(dnum_b, f["cb"], "nt") + dden * f["nb"])
            d_kf = _dot(dqk, qb, "tn") + f["wk"] * dkw
            u_col = f["wk"] * jnp.sum(kf * dkw, axis=-1, keepdims=True)
            z = f["decay"] * (jnp.sum(jnp.sum(c_prev * dcn, axis=-1, keepdims=True), axis=0, keepdims=True)
                              + jnp.sum(n_prev * dnn, axis=-1, keepdims=True))
            r_col = sc * (jnp.sum(dnum * f["num_inter"], axis=-1, keepdims=True) + dden * f["qn"])
            rowsum_p = jnp.sum(p, axis=-1, keepdims=True)
            colsum_p = jnp.sum(p, axis=0, keepdims=True)
            colsum_col = jnp.sum(jnp.where(eye, colsum_p, 0.0), axis=-1, keepdims=True)
            di_col = colsum_col + u_col
            db_col = rowsum_p + r_col - colsum_col - u_col
            db_last = jnp.sum(u_col, axis=0, keepdims=True) + z
            qs = (qb.astype(F32) * sc)
            dc_s[h] = f["decay"] * dcn + _dot(qs.astype(BF16), dnum_b, "tn")
            dn_s[h:h + 1, :] = f["decay"] * dnn + jnp.sum(qs * dden, axis=0, keepdims=True)
            dgate = jnp.where(lane == h, di_col, dgate)
            dgate = jnp.where(lane == 2 * HEADS + h,
                              db_col + jnp.where(rowg == L - 1, db_last, 0.0), dgate)
            dp_ref[:, h * dk:(h + 1) * dk] = d_q.astype(BF16)
            dp_ref[:, qkw + h * dk:qkw + (h + 1) * dk] = (d_kf * kscale).astype(BF16)
            dp_ref[:, D + h * dv:D + (h + 1) * dv] = d_v.astype(BF16)
            dp_ref[:, 2 * D + h * dv:2 * D + (h + 1) * dv] = d_o.astype(BF16)
        rc = jnp.where(lane >= 2 * HEADS, dgate, 0.0)
        d = 1
        while d < L:
            rc = rc + jnp.where(rowg < L - d, pltpu.roll(rc, L - d, 0), 0.0)
            d *= 2
        dlf = pltpu.roll(rc, IF_PAD - HEADS, 1)
        dif = jnp.where(lane < HEADS, dgate, 0.0) + jnp.where(
            (lane >= HEADS) & (lane < 2 * HEADS), dlf * (1.0 - jnp.exp(gc)), 0.0)
        dp_ref[:, 3 * D:3 * D + IF_PAD] = dif.astype(BF16)
        dbif_ref[...] += jnp.sum(dif, axis=0, keepdims=True)

    rev = lambda c: nc - 1 - c
    specs = [pl.BlockSpec(s.block_shape, (lambda c, f=s.index_map: f(rev(c)))) for s in _mlstm_specs(T, D)]
    return pl.pallas_call(
        body, name="mlstm_bwd", grid=(nc,),
        in_specs=specs + [
            pl.BlockSpec((L, IF_PAD), lambda c: (rev(c), 0)),
            pl.BlockSpec((1, 3 * HEADS, L), lambda c: (rev(c), 0, 0)),
            pl.BlockSpec((1, D), lambda c: (0, 0)),
            pl.BlockSpec((L, D), lambda c: (rev(c), 0)),
            pl.BlockSpec((1, HEADS, dk, dv), lambda c: (rev(c), 0, 0, 0)),
            pl.BlockSpec((1, HEADS, dk), lambda c: (rev(c), 0, 0)),
            pl.BlockSpec((1, HEADS, LANES), lambda c: (rev(c), 0, 0)),
        ],
        out_specs=[
            pl.BlockSpec((L, pw), lambda c: (rev(c), 0)),
            pl.BlockSpec((1, D), lambda c: (0, 0)),
            pl.BlockSpec((1, IF_PAD), lambda c: (0, 0)),
        ],
        out_shape=[
            jax.ShapeDtypeStruct((T, pw), BF16),
            jax.ShapeDtypeStruct((1, D), F32),
            jax.ShapeDtypeStruct((1, IF_PAD), F32),
        ],
        scratch_shapes=[pltpu.VMEM((HEADS, dk, dv), F32), pltpu.VMEM((HEADS, dk), F32)],
        compiler_params=_cparams(("arbitrary",)),
    )(proj, proj, proj, proj, gcol, grow, head_norm, dhg, cs, ns, ms)


_GELU_C = math.sqrt(2.0 / math.pi)


def _log_sigmoid(x):
    return jnp.minimum(x, 0.0) - jnp.log(1.0 + jnp.exp(-jnp.abs(x)))


def _rglru_recompute(rec_ref, halo_ref, cw_ref, cb_ref, gw_ref, gb_ref, ap_ref, first, W, Lt):
    x = rec_ref[...]
    halo = jnp.where(first, 0.0, halo_ref[...])
    xe = jnp.concatenate([halo, x], axis=0)
    xs = [pltpu.roll(xe, d, 0)[8:8 + Lt] for d in (3, 2, 1)] + [x]
    cw = cw_ref[...]
    rec_c = cb_ref[...] + cw[0:1] * xs[0] + cw[1:2] * xs[1] + cw[2:3] * xs[2] + cw[3:4] * xs[3]
    gates = _dot(rec_c.astype(BF16), gw_ref[0]) + gb_ref[0]
    r = _sigmoid(gates[:, :W])
    ig = _sigmoid(gates[:, W:])
    lsa = _log_sigmoid(ap_ref[...])
    log_a = R_C * r * lsa
    a = jnp.exp(log_a)
    mult = jnp.sqrt(1.0 - jnp.exp(2.0 * log_a))
    return dict(xs=xs, cw=cw, rec_c=rec_c, r=r, ig=ig, lsa=lsa, a=a, mult=mult)


def _rglru_specs(T, D, Lt, tmap):
    W = D // RBLOCKS
    hb = Lt // 8
    return [
        pl.BlockSpec((Lt, W), lambda g, i: (tmap(i), g)),
        pl.BlockSpec((Lt, W), lambda g, i: (tmap(i), RBLOCKS + g)),
        pl.BlockSpec((8, W), lambda g, i: (jnp.maximum(tmap(i) * hb - 1, 0), RBLOCKS + g)),
        pl.BlockSpec((CONV_W, W), lambda g, i: (0, g)),
        pl.BlockSpec((1, W), lambda g, i: (0, g)),
        pl.BlockSpec((1, W, 2 * W), lambda g, i: (g, 0, 0)),
        pl.BlockSpec((1, 1, 2 * W), lambda g, i: (g, 0, 0)),
        pl.BlockSpec((1, W), lambda g, i: (0, g)),
    ]


def _rglru_fwd(pr, conv_w, conv_b, gate_w, gate_b, a_param):
    T = pr.shape[0]
    D = pr.shape[1] // 2
    W = D // RBLOCKS
    Lt = _pick(T, 256, 8)
    nt = T // Lt

    def body(gate_ref, rec_ref, halo_ref, cw_ref, cb_ref, gw_ref, gb_ref, ap_ref, y_ref, hs_ref, carry):
        i = pl.program_id(1)

        @pl.when(i == 0)
        def _():
            carry[...] = jnp.zeros_like(carry)

        f = _rglru_recompute(rec_ref, halo_ref, cw_ref, cb_ref, gw_ref, gb_ref, ap_ref, i == 0, W, Lt)
        row = lax.broadcasted_iota(jnp.int32, (Lt, W), 0)
        a_c = f["a"]
        u_c = f["mult"] * (f["ig"] * f["rec_c"])
        d = 1
        while d < Lt:
            msk = row >= d
            u_c = jnp.where(msk, a_c * pltpu.roll(u_c, d, 0) + u_c, u_c)
            a_c = jnp.where(msk, a_c * pltpu.roll(a_c, d, 0), a_c)
            d *= 2
        h = u_c + a_c * carry[0:1, :]
        carry[0:1, :] = h[Lt - 1:Lt, :]
        hs_ref[...] = h
        gb = gate_ref[...]
        t = jnp.tanh(_GELU_C * (gb + 0.044715 * gb * gb * gb))
        y_ref[...] = (0.5 * gb * (1.0 + t) * h).astype(BF16)

    blk = pl.BlockSpec((Lt, W), lambda g, i: (i, g))
    return pl.pallas_call(
        body, name="rglru_fwd", grid=(RBLOCKS, nt),
        in_specs=_rglru_specs(T, D, Lt, lambda i: i),
        out_specs=[blk, blk],
        out_shape=[jax.ShapeDtypeStruct((T, D), BF16), jax.ShapeDtypeStruct((T, D), F32)],
        scratch_shapes=[pltpu.VMEM((8, W), F32)],
        compiler_params=_cparams(("parallel", "arbitrary")),
    )(pr, pr, pr, conv_w, conv_b, gate_w, gate_b, a_param)


def _rglru_bwd(pr, hs, dy, conv_w, conv_b, gate_w, gate_b, a_param):
    T = pr.shape[0]
    D = pr.shape[1] // 2
    W = D // RBLOCKS
    Lt = _pick(T, 256, 8)
    nt = T // Lt
    hb = Lt // 8
    tmap = lambda i: nt - 1 - i

    def body(gate_ref, rec_ref, halo_ref, cw_ref, cb_ref, gw_ref, gb_ref, ap_ref, hs_ref, hh_ref, dy_ref,
             dgate_ref, drec_ref, dcw_ref, dcb_ref, dgw_ref, dgb_ref, dap_ref, lam_s, drc_s):
        i = pl.program_id(1)
        ti = nt - 1 - i

        @pl.when(i == 0)
        def _():
            lam_s[...] = jnp.zeros_like(lam_s)
            drc_s[...] = jnp.zeros_like(drc_s)
            dcw_ref[...] = jnp.zeros_like(dcw_ref)
            dcb_ref[...] = jnp.zeros_like(dcb_ref)
            dgw_ref[...] = jnp.zeros_like(dgw_ref)
            dgb_ref[...] = jnp.zeros_like(dgb_ref)
            dap_ref[...] = jnp.zeros_like(dap_ref)

        f = _rglru_recompute(rec_ref, halo_ref, cw_ref, cb_ref, gw_ref, gb_ref, ap_ref, ti == 0, W, Lt)
        a, mult, ig, r, rec_c, lsa, xs, cw = (f[k] for k in ("a", "mult", "ig", "r", "rec_c", "lsa", "xs", "cw"))
        row = lax.broadcasted_iota(jnp.int32, (Lt, W), 0)
        gb = gate_ref[...]
        t = jnp.tanh(_GELU_C * (gb + 0.044715 * gb * gb * gb))
        gel = 0.5 * gb * (1.0 + t)
        dgel = 0.5 * (1.0 + t) + 0.5 * gb * (1.0 - t * t) * _GELU_C * (1.0 + 3.0 * 0.044715 * gb * gb)
        h = hs_ref[...]
        h_first = jnp.where(ti == 0, 0.0, hh_ref[7:8, :])
        h_prev = jnp.where(row == 0, h_first, pltpu.roll(h, 1, 0))
        dy_v = dy_ref[...]
        d_gb = dy_v * h * dgel
        c_c = jnp.where(row < Lt - 1, pltpu.roll(a, Lt - 1, 0), 0.0)
        g_c = dy_v * gel + jnp.where(row == Lt - 1, lam_s[0:1, :], 0.0)
        d = 1
        while d < Lt:
            msk = row < Lt - d
            g_c = jnp.where(msk, c_c * pltpu.roll(g_c, Lt - d, 0) + g_c, g_c)
            c_c = jnp.where(msk, c_c * pltpu.roll(c_c, Lt - d, 0), c_c)
            d *= 2
        lam = g_c
        lam_s[0:1, :] = (a * lam)[0:1, :]
        d_mult = lam * ig * rec_c
        d_loga = lam * h_prev * a - d_mult * (a * a) / mult
        d_ig = lam * mult * rec_c
        d_rec = lam * mult * ig
        d_r = d_loga * (R_C * lsa)
        dap_ref[...] += jnp.sum(d_loga * (R_C * r), axis=0, keepdims=True) * (1.0 - jnp.exp(lsa))
        dgates = jnp.concatenate([d_r * r * (1.0 - r), d_ig * ig * (1.0 - ig)], axis=1)
        dgb_ref[0] += jnp.sum(dgates, axis=0, keepdims=True)
        dgates_b = dgates.astype(BF16)
        dgw_ref[0] += _dot(rec_c.astype(BF16), dgates_b, "tn")
        d_rec = d_rec + _dot(dgates_b, gw_ref[0], "nt")
        dcb_ref[...] += jnp.sum(d_rec, axis=0, keepdims=True)
        dcw_ref[...] += jnp.concatenate(
            [jnp.sum(d_rec * xs[j], axis=0, keepdims=True) for j in range(CONV_W)], axis=0)
        ext = jnp.concatenate([d_rec, drc_s[...]], axis=0)
        up = lambda s: pltpu.roll(ext, Lt + 8 - s, 0)[:Lt]
        d_x = cw[3:4] * d_rec + cw[2:3] * up(1) + cw[1:2] * up(2) + cw[0:1] * up(3)
        drc_s[...] = d_rec[0:8, :]
        dgate_ref[...] = d_gb.astype(BF16)
        drec_ref[...] = d_x.astype(BF16)

    blk = pl.BlockSpec((Lt, W), lambda g, i: (tmap(i), g))
    return pl.pallas_call(
        body, name="rglru_bwd", grid=(RBLOCKS, nt),
        in_specs=_rglru_specs(T, D, Lt, tmap) + [
            blk,
            pl.BlockSpec((8, W), lambda g, i: (jnp.maximum(tmap(i) * hb - 1, 0), g)),
            blk,
        ],
        out_specs=[
            blk,
            blk,
            pl.BlockSpec((CONV_W, W), lambda g, i: (0, g)),
            pl.BlockSpec((1, W), lambda g, i: (0, g)),
            pl.BlockSpec((1, W, 2 * W), lambda g, i: (g, 0, 0)),
            pl.BlockSpec((1, 1, 2 * W), lambda g, i: (g, 0, 0)),
            pl.BlockSpec((1, W), lambda g, i: (0, g)),
        ],
        out_shape=[
            jax.ShapeDtypeStruct((T, D), BF16),
            jax.ShapeDtypeStruct((T, D), BF16),
            jax.ShapeDtypeStruct((CONV_W, D), F32),
            jax.ShapeDtypeStruct((1, D), F32),
            jax.ShapeDtypeStruct((RBLOCKS, W, 2 * W), F32),
            jax.ShapeDtypeStruct((RBLOCKS, 1, 2 * W), F32),
            jax.ShapeDtypeStruct((1, D), F32),
        ],
        scratch_shapes=[pltpu.VMEM((8, W), F32), pltpu.VMEM((8, W), F32)],
        compiler_params=_cparams(("parallel", "arbitrary")),
    )(pr, pr, pr, conv_w, conv_b, gate_w, gate_b, a_param, hs, hs, dy)


def _adamw(name, w, gs, m, v):
    L, R, C = w.shape
    if R % 8 == 0:
        tr, tc = _pick(R, max(8, (1 << 19) // C), 8), C
    else:
        tr, tc = R, _pick(C, max(LANES, (1 << 19) // R), LANES)

    def body(*refs):
        w_ref, m_ref, v_ref = refs[:3]
        g_refs = refs[3:3 + L]
        go_ref, d_ref, nm_ref, nv_ref = refs[3 + L:]
        l = pl.program_id(0)
        gv = g_refs[0][...]
        for q in range(1, L):
            gv = jnp.where(l == q, g_refs[q][...], gv)
        mv = ADAM_B1 * m_ref[0] + (1.0 - ADAM_B1) * gv
        vv = ADAM_B2 * v_ref[0] + (1.0 - ADAM_B2) * (gv * gv)
        m_hat = mv / (1.0 - ADAM_B1 ** ADAM_STEP)
        v_hat = vv / (1.0 - ADAM_B2 ** ADAM_STEP)
        go_ref[0] = gv
        d_ref[0] = -ADAM_LR * (m_hat / (jnp.sqrt(v_hat) + ADAM_EPS) + ADAM_WD * w_ref[0])
        nm_ref[0] = mv
        nv_ref[0] = vv

    blk = pl.BlockSpec((1, tr, tc), lambda l, i, j: (l, i, j))
    g_specs = [pl.BlockSpec((tr, tc), lambda l, i, j, q=q: (jnp.where(l == q, i, 0), jnp.where(l == q, j, 0)))
               for q in range(L)]
    sds = jax.ShapeDtypeStruct((L, R, C), F32)
    return pl.pallas_call(
        body, name=name, grid=(L, R // tr, C // tc),
        in_specs=[blk] * 3 + g_specs, out_specs=[blk] * 4, out_shape=[sds] * 4,
        compiler_params=_cparams(("parallel", "parallel", "parallel")),
    )(w, m, v, *gs)


def _step(x, tgt, p, weights, gather_plan, scatter_plan):
    T, D = x.shape
    pw = 3 * D + IF_PAD
    nc = T // CHUNK
    tm = _pick(T, 1024, 128)
    tkt = _pick(T, 2048, 128)
    td = _pick(D, 1024, 128)
    vec = lambda a, l: a[l:l + 1]
    w = dict(weights)
    g, landed = {}, {}

    def mm(name, a_op, b_op, mode, M, N, K, out_defs, **kw):
        carry = None
        if name in gather_plan:
            carry = gather_plan[name]
        elif name in scatter_plan:
            carry = scatter_plan[name](g)
        outs, extra = _mm(name, a_op, b_op, mode, M, N, K, out_defs, carry=carry, **kw)
        if name in gather_plan:
            for n, a in zip(carry.names, extra):
                w[n] = carry.post[n](a) if n in carry.post else a
        elif carry is not None:
            landed.update(zip(carry.names, extra))
        return outs

    def ffn_fwd(l, h_in):
        dff = w[f"ffn{l}_w_in"].shape[1] // 2
        hn, r = _rms_fwd(f"ffn{l}_norm", h_in, vec(p["norm_ffn"], l))
        tn = _pick(dff, 512, 128)
        gt, ut, act = mm(f"ffn{l}_in", _a_plain(hn, "nn", tm, D), _b_pair(w[f"ffn{l}_w_in"], tn, D), "nn",
                         T, dff, D, [(BF16, dff)] * 3, tm=tm, tn=tn, tk=D, epi=_epi_swiglu)
        tk = _pick(dff, 2816, 128)
        (h_out,) = mm(f"ffn{l}_out", _a_plain(act, "nn", tm, tk), _b_plain(w[f"ffn{l}_w_out"], "nn", td, tk), "nn",
                      T, D, dff, [(F32, D)], tm=tm, tn=td, tk=tk, epi=_epi_resid, extras=[(h_in, td, 0)])
        return h_out, (hn, r, gt, ut, act)

    def ffn_bwd(l, h_in, saved, dh, dhb):
        hn, r, gt, ut, act = saved
        dff = w[f"ffn{l}_w_out"].shape[0]
        tn = _pick(dff, 512, 128)
        dg, du = mm(f"ffn{l}_dact", _a_plain(dhb, "nt", tm, D), _b_plain(w[f"ffn{l}_w_out"], "nt", tn, D), "nt",
                    T, dff, D, [(BF16, dff), (BF16, dff)], tm=tm, tn=tn, tk=D, epi=_epi_swiglu_bwd,
                    extras=[(gt, tn, 0), (ut, tn, 0)])
        tmf, tkf = _pick(dff, 2816, 128), _pick(T, 1024, 128)
        (g[f"ffn{l}_w_out"],) = mm(f"ffn{l}_dwout", _a_plain(act, "tn", tmf, tkf), _b_plain(dhb, "tn", td, tkf), "tn",
                                   dff, D, T, [(BF16, D)], tm=tmf, tn=td, tk=tkf)
        tn2, tk1 = _pick(dff, 2816, 128), _pick(T, 512, 128)
        (g[f"ffn{l}_w_in"],) = mm(f"ffn{l}_dwin", _a_plain(hn, "tn", td, tk1), _b_split_n(dg, du, tn2, tk1), "tn",
                                  D, 2 * dff, T, [(BF16, 2 * dff)], tm=td, tn=tn2, tk=tk1)
        tk2 = _pick(dff, 2048, 128)
        (dhn,) = mm(f"ffn{l}_dx", _a_split_k(dg, du, tm, tk2), _b_plain(w[f"ffn{l}_w_in"], "nt", td, tk2), "nt",
                    T, D, 2 * dff, [(F32, D)], tm=tm, tn=td, tk=tk2)
        dh, dhb, dgn = _rms_bwd(f"ffn{l}_dnorm", h_in, r, dhn, vec(p["norm_ffn"], l), dh)
        return dh, dhb, dgn

    h0 = x
    hn0, r0 = _rms_fwd("mix0_norm", h0, vec(p["norm_mix"], 0))
    tnp = _pick(pw, 1024, 128)
    (proj,) = mm("m_proj", _a_plain(hn0, "nn", tm, D), _b_plain(w["m_w_in"], "nn", tnp, D), "nn",
                 T, pw, D, [(F32, pw)], tm=tm, tn=tnp, tk=D)
    gcol = _mlstm_gates(proj, p["m_b_if"], D)
    grow = gcol[:, :3 * HEADS].reshape(nc, CHUNK, 3 * HEADS).transpose(0, 2, 1)
    carry = gather_plan.get("mlstm_fwd")
    (hg, cs, ns, ms), extra = _mlstm_fwd(proj, gcol, grow, p["m_head_norm"], carry)
    if carry is not None:
        for n, a in zip(carry.names, extra):
            w[n] = carry.post[n](a) if n in carry.post else a
    (h1,) = mm("m_out", _a_plain(hg, "nn", tm, D), _b_plain(w["m_w_out"], "nn", td, D), "nn",
               T, D, D, [(F32, D)], tm=tm, tn=td, tk=D, epi=_epi_resid, extras=[(h0, td, 0)])
    h2, ffn0 = ffn_fwd(0, h1)
    hn2, r2 = _rms_fwd("mix1_norm", h2, vec(p["norm_mix"], 1))
    (pr,) = mm("r_proj", _a_plain(hn2, "nn", tm, D), _b_plain(w["r_w_in"], "nn", td, D), "nn",
               T, 2 * D, D, [(F32, 2 * D)], tm=tm, tn=td, tk=D)
    y, hs = _rglru_fwd(pr, p["conv_w"], p["conv_b"], w["gate_w"], p["gate_b"], p["a_param"])
    (h3,) = mm("r_out", _a_plain(y, "nn", tm, D), _b_plain(w["r_w_out"], "nn", td, D), "nn",
               T, D, D, [(F32, D)], tm=tm, tn=td, tk=D, epi=_epi_resid, extras=[(h2, td, 0)])
    h4, ffn1 = ffn_fwd(1, h3)
    loss, dh, dhb, g["norm_final"] = _loss_head(h4, p["norm_final"], tgt)

    dh, dhb, dnf1 = ffn_bwd(1, h3, ffn1, dh, dhb)
    (dy,) = mm("r_dy", _a_plain(dhb, "nt", tm, D), _b_plain(w["r_w_out"], "nt", td, D), "nt",
               T, D, D, [(F32, D)], tm=tm, tn=td, tk=D)
    (g["r_w_out"],) = mm("r_dwout", _a_plain(y, "tn", td, tkt), _b_plain(dhb, "tn", td, tkt), "tn",
                         D, D, T, [(BF16, D)], tm=td, tn=td, tk=tkt)
    dgate, drec, g["conv_w"], g["conv_b"], dgw, g["gate_b"], g["a_param"] = _rglru_bwd(
        pr, hs, dy, p["conv_w"], p["conv_b"], w["gate_w"], p["gate_b"], p["a_param"])
    g["gate_w"] = dgw.astype(BF16)
    (g["r_w_in"],) = mm("r_dwin", _a_plain(hn2, "tn", td, tkt), _b_split_n(dgate, drec, td, tkt), "tn",
                        D, 2 * D, T, [(BF16, 2 * D)], tm=td, tn=td, tk=tkt)
    (dhn2,) = mm("r_dx", _a_split_k(dgate, drec, tm, td), _b_plain(w["r_w_in"], "nt", td, td), "nt",
                 T, D, 2 * D, [(F32, D)], tm=tm, tn=td, tk=td)
    dh, dhb, dnm1 = _rms_bwd("mix1_dnorm", h2, r2, dhn2, vec(p["norm_mix"], 1), dh)
    dh, dhb, dnf0 = ffn_bwd(0, h1, ffn0, dh, dhb)
    (dhg,) = mm("m_dhg", _a_plain(dhb, "nt", tm, D), _b_plain(w["m_w_out"], "nt", td, D), "nt",
                T, D, D, [(F32, D)], tm=tm, tn=td, tk=D)
    (g["m_w_out"],) = mm("m_dwout", _a_plain(hg, "tn", td, tkt), _b_plain(dhb, "tn", td, tkt), "tn",
                         D, D, T, [(BF16, D)], tm=td, tn=td, tk=tkt)
    dproj, g["m_head_norm"], g["m_b_if"] = _mlstm_bwd(proj, gcol, grow, p["m_head_norm"], dhg, cs, ns, ms)
    (g["m_w_in"],) = mm("m_dwin", _a_plain(hn0, "tn", td, tkt), _b_plain(dproj, "tn", tnp, tkt), "tn",
                        D, pw, T, [(BF16, pw)], tm=td, tn=tnp, tk=tkt)
    (dhn0,) = mm("m_dx", _a_plain(dproj, "nt", tm, tnp), _b_plain(w["m_w_in"], "nt", td, tnp), "nt",
                 T, D, pw, [(F32, D)], tm=tm, tn=td, tk=tnp)
    grad_x, _, dnm0 = _rms_bwd("mix0_dnorm", h0, r0, dhn0, vec(p["norm_mix"], 0), dh)
    g["norm_mix"] = jnp.concatenate([dnm0, dnm1], axis=0)
    g["norm_ffn"] = jnp.concatenate([dnf0, dnf1], axis=0)
    return loss, grad_x, g, landed


_MESH = pl.DeviceIdType.MESH
_HBM = pl.BlockSpec(memory_space=pltpu.HBM)
N_CHIPS = 4


def _place():
    x, y, c = lax.axis_index("x"), lax.axis_index("y"), lax.axis_index("c")
    chips = [(1 - x, y), (x, 1 - y), (1 - x, 1 - y)]
    return x, y, c, 2 * x + y, chips, [2 * px + py for px, py in chips]


class _Piece:
    def __init__(self, kind, rows, cols):
        self.kind, self.rows, self.cols, self.hr = kind, rows, cols, rows // 2

    def full_shape(self):
        return {"col": (self.rows, N_CHIPS * self.cols), "row": (N_CHIPS * self.rows, self.cols),
                "stack": (N_CHIPS, self.rows, self.cols)}[self.kind]

    def region(self, ref, s, h):
        rows = pl.ds(h * self.hr, self.hr)
        if self.kind == "col":
            return ref.at[rows, pl.ds(pl.multiple_of(s * self.cols, LANES), self.cols)]
        if self.kind == "row":
            return ref.at[pl.ds(pl.multiple_of(s * self.rows + h * self.hr, 16), self.hr), :]
        return ref.at[s, rows, :]

    def shard_half(self, ref, h):
        return ref.at[pl.ds(h * self.hr, self.hr), :]


class _Gather:
    def __init__(self, names, shards, pieces, post=None):
        n = len(names)
        self.names, self.ins, self.pieces, self.post = list(names), list(shards), list(pieces), post or {}
        self.out_shapes = [jax.ShapeDtypeStruct(p.full_shape(), s.dtype) for p, s in zip(pieces, shards)]
        self.scratch = [pltpu.SemaphoreType.DMA((n, 2))] + [pltpu.SemaphoreType.DMA((n, 3))] * 4

    def _copies(self, ins, outs, sems):
        loc, snd, rcv, fsnd, frcv = sems
        x, y, c, j, chips, cj = _place()

        def local(w, h):
            return pltpu.make_async_copy(self.pieces[w].shard_half(ins[w], h), self.pieces[w].region(outs[w], j, h),
                                         loc.at[w, h])

        def ici_send(w, k):
            return pltpu.make_async_remote_copy(
                src_ref=self.pieces[w].shard_half(ins[w], c), dst_ref=self.pieces[w].region(outs[w], j, c),
                send_sem=snd.at[w, k], recv_sem=rcv.at[w, k], device_id=(*chips[k], c), device_id_type=_MESH)

        def ici_recv(w, k):
            region = self.pieces[w].region(outs[w], cj[k], c)
            return pltpu.make_async_remote_copy(
                src_ref=region, dst_ref=region, send_sem=snd.at[w, k], recv_sem=rcv.at[w, k],
                device_id=(*chips[k], c), device_id_type=_MESH)

        def d2d(w, k, h):
            region = self.pieces[w].region(outs[w], cj[k], h)
            return pltpu.make_async_remote_copy(
                src_ref=region, dst_ref=region, send_sem=fsnd.at[w, k], recv_sem=frcv.at[w, k],
                device_id=(x, y, 1 - c), device_id_type=_MESH)

        return c, local, ici_send, ici_recv, d2d

    def start(self, ins, outs, sems):
        _, local, ici_send, _, _ = self._copies(ins, outs, sems)
        for w in range(len(self.names)):
            for k in range(3):
                ici_send(w, k).start()
        for w in range(len(self.names)):
            for h in range(2):
                local(w, h).start()

    def finish(self, ins, outs, sems):
        c, local, ici_send, ici_recv, d2d = self._copies(ins, outs, sems)
        n = len(self.names)
        for w in range(n):
            for k in range(3):
                ici_recv(w, k).wait_recv()
                d2d(w, k, c).start()
        for w in range(n):
            for k in range(3):
                d2d(w, k, 1 - c).wait_recv()
        for w in range(n):
            for k in range(3):
                ici_send(w, k).wait_send()
                d2d(w, k, c).wait_send()
            for h in range(2):
                local(w, h).wait()


class _Scatter:
    def __init__(self, names, grads, pieces):
        n = len(names)
        self.names, self.ins, self.pieces = list(names), list(grads), list(pieces)
        self.out_shapes = [jax.ShapeDtypeStruct((7, p.hr, p.cols), g.dtype) for p, g in zip(pieces, grads)]
        self.scratch = [pltpu.SemaphoreType.DMA((n, 7))] * 2

    def _copy(self, ins, outs, sems, w, k):
        x, y, c = lax.axis_index("x"), lax.axis_index("y"), lax.axis_index("c")
        px, py, pc = ((1 - x) if k & 4 else x, (1 - y) if k & 2 else y, (1 - c) if k & 1 else c)
        return pltpu.make_async_remote_copy(
            src_ref=self.pieces[w].region(ins[w], 2 * px + py, pc), dst_ref=outs[w].at[k - 1],
            send_sem=sems[0].at[w, k - 1], recv_sem=sems[1].at[w, k - 1], device_id=(px, py, pc),
            device_id_type=_MESH)

    def start(self, ins, outs, sems):
        for w in range(len(self.names)):
            for k in range(1, 8):
                self._copy(ins, outs, sems, w, k).start()

    def finish(self, ins, outs, sems):
        for w in range(len(self.names)):
            for k in range(1, 8):
                self._copy(ins, outs, sems, w, k).wait()


def _run_exchange(name, ex):
    ni, no = len(ex.ins), len(ex.out_shapes)

    def body(*refs):
        ins, outs, sems = refs[:ni], refs[ni:ni + no], refs[ni + no:]
        ex.start(ins, outs, sems)
        ex.finish(ins, outs, sems)

    return pl.pallas_call(
        body, name=name, in_specs=[_HBM] * ni, out_specs=[_HBM] * no, out_shape=list(ex.out_shapes),
        scratch_shapes=list(ex.scratch),
    )(*ex.ins)


def _rs_sum(name, piece, grad, landed, jc_idx):
    hr, C = piece.hr, piece.cols
    br = _pick(hr, max(16, (1 << 18) // C), 16)
    nb = hr // br
    if piece.kind == "col":
        g_spec = pl.BlockSpec((br, C), lambda i, jc: (jc[1] * nb + i, jc[0]))
    elif piece.kind == "row":
        g_spec = pl.BlockSpec((br, C), lambda i, jc: ((jc[0] * 2 + jc[1]) * nb + i, 0))
    else:
        g_spec = pl.BlockSpec((None, br, C), lambda i, jc: (jc[0], jc[1] * nb + i, 0))

    def body(jc_ref, g_ref, l_ref, o_ref):
        acc = g_ref[...].astype(F32)
        for k in range(7):
            acc = acc + l_ref[k].astype(F32)
        o_ref[0] = acc

    return pl.pallas_call(
        body, name=name,
        grid_spec=pltpu.PrefetchScalarGridSpec(
            num_scalar_prefetch=1, grid=(nb,),
            in_specs=[g_spec, pl.BlockSpec((7, br, C), lambda i, jc: (0, i, 0))],
            out_specs=pl.BlockSpec((1, br, C), lambda i, jc: (jc[1], i, 0))),
        out_shape=jax.ShapeDtypeStruct((2, hr, C), F32),
        compiler_params=_cparams(("parallel",)),
    )(jc_idx, grad, landed)


def _rs_share_halves(rs):
    n = len(rs)

    def body(*refs):
        outs = refs[n:2 * n]
        send_sems, recv_sems = refs[2 * n:]
        x, y, c, _, _, _ = _place()
        cps = []
        for w in range(n):
            cps.append(pltpu.make_async_remote_copy(
                src_ref=outs[w].at[c], dst_ref=outs[w].at[c], send_sem=send_sems.at[w],
                recv_sem=recv_sems.at[w], device_id=(x, y, 1 - c), device_id_type=_MESH))
            cps[-1].start()
        for w in range(n):
            cps[w].wait_send()
            pltpu.make_async_remote_copy(
                src_ref=outs[w].at[1 - c], dst_ref=outs[w].at[1 - c], send_sem=send_sems.at[w],
                recv_sem=recv_sems.at[w], device_id=(x, y, 1 - c), device_id_type=_MESH).wait_recv()

    return pl.pallas_call(
        body, name="rs_share_halves",
        in_specs=[_HBM] * n, out_specs=[_HBM] * n,
        out_shape=[jax.ShapeDtypeStruct(a.shape, a.dtype) for a in rs],
        input_output_aliases={w: w for w in range(n)},
        scratch_shapes=[pltpu.SemaphoreType.DMA((n,))] * 2,
    )(*rs)


def _allreduce_small(v):
    R = v.shape[0]

    def body(v_ref, o_ref, slots, send_sems, recv_sems):
        x, y, c = lax.axis_index("x"), lax.axis_index("y"), lax.axis_index("c")
        me = 4 * x + 2 * y + c
        cps = []
        for k in range(1, 8):
            peer = ((1 - x) if k & 4 else x, (1 - y) if k & 2 else y, (1 - c) if k & 1 else c)
            cps.append(pltpu.make_async_remote_copy(
                src_ref=v_ref, dst_ref=slots.at[k - 1], send_sem=send_sems.at[k - 1],
                recv_sem=recv_sems.at[k - 1], device_id=peer, device_id_type=_MESH))
            cps[-1].start()
        for cp in cps:
            cp.wait()
        acc = jnp.zeros((R, LANES), F32)
        for d in range(8):
            k = ((d // 4) ^ x) * 4 + (((d // 2) % 2) ^ y) * 2 + ((d % 2) ^ c)
            other = slots[jnp.maximum(k - 1, 0)]
            acc = acc + jnp.where(d == me, v_ref[...], other)
        o_ref[...] = acc

    vm = pl.BlockSpec(memory_space=pltpu.VMEM)
    return pl.pallas_call(
        body, name="allreduce_small",
        in_specs=[vm], out_specs=vm,
        out_shape=jax.ShapeDtypeStruct((R, LANES), F32),
        scratch_shapes=[pltpu.VMEM((7, R, LANES), F32), pltpu.SemaphoreType.DMA((7,)), pltpu.SemaphoreType.DMA((7,))],
    )(v)


def _rows(a):
    flat = a.reshape(-1)
    return jnp.pad(flat, (0, (-flat.shape[0]) % LANES)).reshape(-1, LANES)


def _pack_rows(parts, mult=8):
    v = jnp.concatenate([_rows(a) for a in parts], axis=0)
    return jnp.pad(v, ((0, (-v.shape[0]) % mult), (0, 0)))


def _unpack_rows(v, shapes):
    out, r = [], 0
    for s in shapes:
        size = math.prod(s)
        nr = -(-size // LANES)
        out.append(v[r:r + nr].reshape(-1)[:size].reshape(s))
        r += nr
    return out


def kernel(x, norm_mix, norm_ffn, norm_final, m_w_in, m_b_if, m_head_norm, m_w_out, r_w_in, r_conv_w, r_conv_b, r_gate_w, r_gate_b, r_a_param, r_w_out, ffn_w_in, ffn_w_out, loss_target, m_norm_mix, m_norm_ffn, m_norm_final, m_m_w_in, m_m_b_if, m_m_head_norm, m_m_w_out, m_r_w_in, m_r_conv_w, m_r_conv_b, m_r_gate_w, m_r_gate_b, m_r_a_param, m_r_w_out, m_ffn_w_in, m_ffn_w_out, v_norm_mix, v_norm_ffn, v_norm_final, v_m_w_in, v_m_b_if, v_m_head_norm, v_m_w_out, v_r_w_in, v_r_conv_w, v_r_conv_b, v_r_gate_w, v_r_gate_b, v_r_a_param, v_r_w_out, v_ffn_w_in, v_ffn_w_out):
    names = ["norm_mix", "norm_ffn", "norm_final", "m_w_in", "m_b_if", "m_head_norm", "m_w_out", "r_w_in", "r_conv_w",
             "r_conv_b", "r_gate_w", "r_gate_b", "r_a_param", "r_w_out", "ffn_w_in", "ffn_w_out"]
    w = dict(zip(names, [norm_mix, norm_ffn, norm_final, m_w_in, m_b_if, m_head_norm, m_w_out, r_w_in, r_conv_w,
                         r_conv_b, r_gate_w, r_gate_b, r_a_param, r_w_out, ffn_w_in, ffn_w_out]))
    mom = dict(zip(names, [m_norm_mix, m_norm_ffn, m_norm_final, m_m_w_in, m_m_b_if, m_m_head_norm, m_m_w_out, m_r_w_in,
                           m_r_conv_w, m_r_conv_b, m_r_gate_w, m_r_gate_b, m_r_a_param, m_r_w_out, m_ffn_w_in, m_ffn_w_out]))
    var = dict(zip(names, [v_norm_mix, v_norm_ffn, v_norm_final, v_m_w_in, v_m_b_if, v_m_head_norm, v_m_w_out, v_r_w_in,
                           v_r_conv_w, v_r_conv_b, v_r_gate_w, v_r_gate_b, v_r_a_param, v_r_w_out, v_ffn_w_in, v_ffn_w_out]))
    D = x.shape[-1]
    W = D // RBLOCKS
    nproj = m_w_in.shape[2] * N_CHIPS
    shard_col = lax.axis_index("x") * 2 + lax.axis_index("y")
    jc_idx = jnp.stack([shard_col, lax.axis_index("c")]).astype(jnp.int32)

    def cols(g, lead):
        nl = len(lead)
        g = g.reshape((N_CHIPS,) + lead + (-1,))
        g = jnp.moveaxis(g, 0, nl)
        return g.reshape(lead + (-1,))

    def uncols(a, lead):
        nl = len(lead)
        a = a.reshape(lead + (N_CHIPS, -1))
        a = jnp.moveaxis(a, nl, 0)
        return a.reshape(N_CHIPS, math.prod(lead), -1)

    shard = {"m_w_in": m_w_in[0], "m_w_out": m_w_out[0], "r_w_in": r_w_in[0],
             "gate_w": r_gate_w[0].reshape(-1, r_gate_w.shape[-1]), "r_w_out": r_w_out[0]}
    kind = {"m_w_in": "stack", "m_w_out": "row", "r_w_in": "col", "gate_w": "stack", "r_w_out": "row"}
    for l in range(2):
        shard[f"ffn{l}_w_in"], kind[f"ffn{l}_w_in"] = ffn_w_in[l], "col"
        shard[f"ffn{l}_w_out"], kind[f"ffn{l}_w_out"] = ffn_w_out[l], "row"
    shard = {n: a.astype(BF16) for n, a in shard.items()}
    piece = {n: _Piece(kind[n], *shard[n].shape) for n in shard}
    small_sharded = ["r_conv_w", "r_conv_b", "r_gate_b", "r_a_param"]
    small_pack = _pack_rows([w[n] for n in small_sharded], mult=16)
    post = {"m_w_in": lambda a: jnp.pad(cols(a, (D,)), ((0, 0), (0, 3 * D + IF_PAD - nproj))),
            "gate_w": lambda a: cols(a, (RBLOCKS, W))}

    def gather(names):
        return _Gather(names, [shard[n] for n in names], [piece[n] for n in names], post)

    first = _Gather(["m_w_in", "small"], [shard["m_w_in"], small_pack],
                    [piece["m_w_in"], _Piece("stack", *small_pack.shape)], post)
    got = dict(zip(first.names, _run_exchange("gather_first", first)))
    weights = {"m_w_in": post["m_w_in"](got["m_w_in"])}
    gather_plan = {"m_proj": gather(["ffn0_w_in"]),
                   "mlstm_fwd": gather(["m_w_out", "r_w_in", "gate_w", "r_w_out"]),
                   "ffn0_in": gather(["ffn0_w_out", "ffn1_w_in"]), "ffn0_out": gather(["ffn1_w_out"])}
    sp = got["small"]
    ws = D // N_CHIPS
    wg = 2 * W // N_CHIPS
    r1, r2, r3 = CONV_W * ws // LANES, (CONV_W + 1) * ws // LANES, ((CONV_W + 1) * ws + RBLOCKS * wg) // LANES
    conv_w_f = sp[:, :r1].reshape(N_CHIPS, CONV_W, ws).transpose(1, 0, 2).reshape(CONV_W, D)
    conv_b_f = sp[:, r1:r2].reshape(1, D)
    gate_b_f = sp[:, r2:r3].reshape(N_CHIPS, RBLOCKS, wg).transpose(1, 0, 2).reshape(RBLOCKS, 1, 2 * W)
    a_param_f = sp[:, r3:r3 + ws // LANES].reshape(1, D)
    p = dict(
        norm_mix=norm_mix, norm_ffn=norm_ffn, norm_final=norm_final.reshape(1, D),
        m_b_if=jnp.pad(m_b_if, ((0, 0), (0, IF_PAD - m_b_if.shape[1]))), m_head_norm=m_head_norm,
        conv_w=conv_w_f, conv_b=conv_b_f, gate_b=gate_b_f, a_param=a_param_f,
    )

    gsrc = {}

    def scatter(names):
        def make(g):
            for n in names:
                if n == "m_w_in":
                    gsrc[n] = uncols(g[n][:, :nproj], (D,))
                elif n == "gate_w":
                    gsrc[n] = uncols(g[n], (RBLOCKS, W))
                else:
                    gsrc[n] = g[n]
            return _Scatter(names, [gsrc[n] for n in names], [piece[n] for n in names])
        return make

    scatter_plan = {"ffn1_dwin": scatter(["ffn1_w_out"]), "ffn1_dx": scatter(["ffn1_w_in"]),
                    "r_dx": scatter(["r_w_out", "gate_w"]), "ffn0_dact": scatter(["r_w_in"]),
                    "ffn0_dwin": scatter(["ffn0_w_out"]), "ffn0_dx": scatter(["ffn0_w_in"]),
                    "m_dwin": scatter(["m_w_out"]), "m_dx": scatter(["m_w_in"])}
    loss_part, grad_x, g, landed = _step(x[0], loss_target[0], p, weights, gather_plan, scatter_plan)

    big = list(shard)
    halves = [_rs_sum(f"rs_sum_{n}", piece[n], gsrc[n], landed[n], jc_idx) for n in big]
    reduced = {n: r.reshape(shard[n].shape) for n, r in zip(big, _rs_share_halves(halves))}

    small_all = ["norm_mix", "norm_ffn", "norm_final", "m_b_if", "m_head_norm", "r_conv_w", "r_conv_b", "r_gate_b", "r_a_param"]
    gsmall = [g["norm_mix"], g["norm_ffn"], g["norm_final"], g["m_b_if"], g["m_head_norm"], g["conv_w"], g["conv_b"],
              g["gate_b"].reshape(RBLOCKS, 2 * W), g["a_param"], loss_part]
    summed = _unpack_rows(_allreduce_small(_pack_rows(gsmall)), [a.shape for a in gsmall])
    loss = summed[-1][0, 0]
    gs = dict(zip(small_all, summed[:-1]))
    gs["norm_final"] = gs["norm_final"].reshape(D)
    gs["m_b_if"] = gs["m_b_if"][:, :m_b_if.shape[1]]
    gs["r_conv_w"] = lax.dynamic_slice_in_dim(gs["r_conv_w"], shard_col * ws, ws, axis=1).reshape(r_conv_w.shape)
    gs["r_conv_b"] = lax.dynamic_slice_in_dim(gs["r_conv_b"], shard_col * ws, ws, axis=1).reshape(r_conv_b.shape)
    gs["r_gate_b"] = lax.dynamic_slice_in_dim(gs["r_gate_b"], shard_col * (2 * W // N_CHIPS), 2 * W // N_CHIPS,
                                              axis=1).reshape(r_gate_b.shape)
    gs["r_a_param"] = lax.dynamic_slice_in_dim(gs["r_a_param"], shard_col * ws, ws, axis=1).reshape(r_a_param.shape)

    grads, delta, new_m, new_v = {}, {}, {}, {}
    layers = {"m_w_in": ["m_w_in"], "m_w_out": ["m_w_out"], "r_w_in": ["r_w_in"], "r_gate_w": ["gate_w"],
              "r_w_out": ["r_w_out"], "ffn_w_in": ["ffn0_w_in", "ffn1_w_in"], "ffn_w_out": ["ffn0_w_out", "ffn1_w_out"]}
    for n, parts in layers.items():
        shp = w[n].shape
        if n == "m_w_in":
            tr_ = lambda a: jnp.swapaxes(a, -1, -2)
            res = _adamw(f"adamw_{n}", tr_(w[n]), [tr_(reduced[n])], tr_(mom[n]), tr_(var[n]))
            grads[n], delta[n], new_m[n], new_v[n] = (tr_(a) for a in res)
            continue
        as3d = lambda a: a.reshape((len(parts),) + shard[parts[0]].shape)
        res = _adamw(f"adamw_{n}", as3d(w[n]), [reduced[q] for q in parts], as3d(mom[n]), as3d(var[n]))
        grads[n], delta[n], new_m[n], new_v[n] = (a.reshape(shp) for a in res)

    packs = [_pack_rows([t[n] for n in small_all])[None] for t in (w, gs, mom, var)]
    outs = _adamw("adamw_small", packs[0], [packs[1][0]], packs[2], packs[3])
    for dst, pk in zip((delta, new_m, new_v), outs[1:]):
        for n, a in zip(small_all, _unpack_rows(pk[0], [w[n].shape for n in small_all])):
            dst[n] = a
    for n in small_all:
        grads[n] = gs[n]
    return (loss, grad_x[None], *[grads[n] for n in names], *[delta[n] for n in names],
            *[new_m[n] for n in names], *[new_v[n] for n in names])
```

```python
import functools
import math

import jax
import jax.numpy as jnp
from jax import lax
from jax.experimental import pallas as pl
from jax.experimental.pallas import tpu as pltpu

F32 = jnp.float32
BF16 = jnp.bfloat16

EPS = 1e-6
HEADS = 8
CHUNK = 256
RBLOCKS = 8
CONV_W = 4
R_C = 8.0
LANES = 128
IF_PAD = LANES
VMEM_LIMIT = 52 << 20

ADAM_LR = 0.001
ADAM_B1 = 0.9
ADAM_B2 = 0.999
ADAM_EPS = 1e-08
ADAM_WD = 0.01
ADAM_STEP = 10

_DIMS = {
    "nn": (((1,), (0,)), ((), ())),
    "nt": (((1,), (1,)), ((), ())),
    "tn": (((0,), (0,)), ((), ())),
}


def _dot(a, b, mode="nn"):
    return lax.dot_general(a, b, _DIMS[mode], preferred_element_type=F32)


def _pick(n, pref, mult):
    best = None
    d = mult
    while d <= min(n, pref):
        if n % d == 0:
            best = d
        d += mult
    return best if best is not None else n


def _cparams(sem):
    return pltpu.CompilerParams(dimension_semantics=sem, vmem_limit_bytes=VMEM_LIMIT)


def _a_plain(a, mode, tm, tk):
    if mode == "tn":
        return [(a, (tk, tm), lambda i, j, k: (k, i))], lambda t, i, j, k: t[0]
    return [(a, (tm, tk), lambda i, j, k: (i, k))], lambda t, i, j, k: t[0]


def _a_split_k(a1, a2, tm, tk):
    h = a1.shape[1] // tk
    return ([(a1, (tm, tk), lambda i, j, k: (i, jnp.minimum(k, h - 1))),
             (a2, (tm, tk), lambda i, j, k: (i, jnp.maximum(k - h, 0)))],
            lambda t, i, j, k: jnp.where(k < h, t[0], t[1]))


def _sigmoid(x):
    return 1.0 / (1.0 + jnp.exp(-x))


def _epi_swiglu(acc):
    half = acc.shape[1] // 2
    g, u = acc[:, :half], acc[:, half:]
    return g, u, g * _sigmoid(g) * u


def _b_plain(b, mode, tn, tk):
    if mode == "nt":
        return [(b, (tn, tk), lambda i, j, k: (j, k))], lambda t, i, j, k: t[0]
    return [(b, (tk, tn), lambda i, j, k: (k, j))], lambda t, i, j, k: t[0]


def _b_split_n(b1, b2, tn, tk):
    h = b1.shape[1] // tn
    return ([(b1, (tk, tn), lambda i, j, k: (jnp.where(j < h, k, 0), jnp.minimum(j, h - 1))),
             (b2, (tk, tn), lambda i, j, k: (jnp.where(j < h, 0, k), jnp.maximum(j - h, 0)))],
            lambda t, i, j, k: jnp.where(j < h, t[0], t[1]))


def _b_pair(b, tn, tk):
    off = b.shape[1] // 2 // tn
    return ([(b, (tk, tn), lambda i, j, k: (k, j)), (b, (tk, tn), lambda i, j, k: (k, j + off))],
            lambda t, i, j, k: jnp.concatenate(t, axis=1))


def _mm(name, a_op, b_op, mode, M, N, K, out_defs, *, tm, tn, tk, epi=None, extras=(), carry=None):
    assert M % tm == 0 and N % tn == 0 and K % tk == 0, (name, M, N, K, tm, tn, tk)
    (a_in, a_fn), (b_in, b_fn) = a_op, b_op
    ni, nj, nk = M // tm, N // tn, K // tk
    na, nb, ne, no = len(a_in), len(b_in), len(extras), len(out_defs)
    nci = len(carry.ins) if carry else 0
    nco = len(carry.out_shapes) if carry else 0
    ncs = len(carry.scratch) if carry else 0

    def body(*refs):
        pos = 0

        def take(n):
            nonlocal pos
            r = refs[pos:pos + n]
            pos += n
            return r

        a_refs, b_refs, ex, c_in, outs, c_out = take(na), take(nb), take(ne), take(nci), take(no), take(nco)
        acc_ref = take(1)[0] if nk > 1 else None
        c_scr = take(ncs)
        i, j, k = pl.program_id(0), pl.program_id(1), pl.program_id(2)
        if carry:
            @pl.when((i == 0) & (j == 0) & (k == 0))
            def _():
                carry.start(c_in, c_out, c_scr)

        def finish(acc):
            res = epi(acc, *[e[...] for e in ex]) if epi is not None else (acc,)
            for o_ref, r in zip(outs, res):
                o_ref[...] = r.astype(o_ref.dtype)

        av = a_fn([r[...] for r in a_refs], i, j, k).astype(BF16)
        bv = b_fn([r[...] for r in b_refs], i, j, k).astype(BF16)
        d = _dot(av, bv, mode)
        if nk == 1:
            finish(d)
        else:
            @pl.when(k == 0)
            def _():
                acc_ref[...] = d

            @pl.when(k > 0)
            def _():
                acc_ref[...] += d

            @pl.when(k == nk - 1)
            def _():
                finish(acc_ref[...])
        if carry:
            @pl.when((i == ni - 1) & (j == nj - 1) & (k == nk - 1))
            def _():
                carry.finish(c_in, c_out, c_scr)

    def spec(shape, imap):
        return pl.BlockSpec(shape, imap)

    in_specs = [spec(s, m) for _, s, m in a_in + b_in]
    in_specs += [pl.BlockSpec((tm, w), lambda i, j, k, off=off: (i, j + off)) for _, w, off in extras]
    out_specs = [pl.BlockSpec((tm, c * tn // N), lambda i, j, k: (i, j)) for _, c in out_defs]
    out_shape = [jax.ShapeDtypeStruct((M, c), dt) for dt, c in out_defs]
    scratch = [pltpu.VMEM((tm, tn), F32)] if nk > 1 else []
    operands = [x[0] for x in a_in + b_in] + [e[0] for e in extras]
    if carry:
        in_specs += [_HBM] * nci
        out_specs += [_HBM] * nco
        out_shape += list(carry.out_shapes)
        scratch += list(carry.scratch)
        operands += list(carry.ins)
    sem = ("arbitrary",) * 3 if carry else ("parallel", "parallel", "arbitrary")
    res = pl.pallas_call(
        body, name=name, grid=(ni, nj, nk), in_specs=in_specs, out_specs=out_specs, out_shape=out_shape,
        scratch_shapes=scratch, compiler_params=_cparams(sem),
    )(*operands)
    return res[:no], res[no:]


def _epi_resid(acc, resid):
    return (resid + acc,)


def _epi_swiglu_bwd(da, g, u):
    g = g.astype(F32)
    u = u.astype(F32)
    sig = _sigmoid(g)
    return da * u * (sig * (1.0 + g * (1.0 - sig))), da * (g * sig)


def _rms_fwd(name, x, g):
    T, D = x.shape
    tr = _pick(T, 512, 16)

    def body(x_ref, g_ref, hn_ref, r_ref):
        xv = x_ref[...]
        r = lax.rsqrt(jnp.mean(xv * xv, axis=-1, keepdims=True) + EPS)
        hn_ref[...] = (xv * r * g_ref[...]).astype(BF16)
        r_ref[...] = r

    return pl.pallas_call(
        body, name=name, grid=(T // tr,),
        in_specs=[pl.BlockSpec((tr, D), lambda i: (i, 0)), pl.BlockSpec((1, D), lambda i: (0, 0))],
        out_specs=[pl.BlockSpec((tr, D), lambda i: (i, 0)), pl.BlockSpec((tr, 1), lambda i: (i, 0))],
        out_shape=[jax.ShapeDtypeStruct((T, D), BF16), jax.ShapeDtypeStruct((T, 1), F32)],
        compiler_params=_cparams(("parallel",)),
    )(x, g)


def _rms_bwd(name, x, r, dy, g, dres):
    T, D = x.shape
    tr = _pick(T, 512, 16)

    def body(x_ref, r_ref, dy_ref, g_ref, dres_ref, dh_ref, dhb_ref, dg_ref):
        xh = x_ref[...] * r_ref[...]
        dy_v = dy_ref[...]
        dyg = dy_v * g_ref[...]
        m = jnp.mean(dyg * xh, axis=-1, keepdims=True)
        dh = dres_ref[...] + r_ref[...] * (dyg - xh * m)
        dh_ref[...] = dh
        dhb_ref[...] = dh.astype(BF16)

        @pl.when(pl.program_id(0) == 0)
        def _():
            dg_ref[...] = jnp.zeros_like(dg_ref)

        dg_ref[...] += jnp.sum(dy_v * xh, axis=0, keepdims=True)

    row = pl.BlockSpec((tr, D), lambda i: (i, 0))
    vec = pl.BlockSpec((1, D), lambda i: (0, 0))
    return pl.pallas_call(
        body, name=name, grid=(T // tr,),
        in_specs=[row, pl.BlockSpec((tr, 1), lambda i: (i, 0)), row, vec, row],
        out_specs=[row, row, vec],
        out_shape=[jax.ShapeDtypeStruct((T, D), F32), jax.ShapeDtypeStruct((T, D), BF16),
                   jax.ShapeDtypeStruct((1, D), F32)],
        compiler_params=_cparams(("arbitrary",)),
    )(x, r, dy, g, dres)


def _loss_head(x, g, tgt):
    T, D = x.shape
    tr = _pick(T, 512, 16)

    def body(x_ref, g_ref, t_ref, loss_ref, dh_ref, dhb_ref, dg_ref):
        xv = x_ref[...]
        gv = g_ref[...]
        r = lax.rsqrt(jnp.mean(xv * xv, axis=-1, keepdims=True) + EPS)
        xh = xv * r
        e = xh * gv - t_ref[...]
        part = 0.5 * jnp.sum(jnp.mean(e * e, axis=-1, keepdims=True), axis=0, keepdims=True)
        dy = e * (1.0 / D)
        dyg = dy * gv
        m = jnp.mean(dyg * xh, axis=-1, keepdims=True)
        dh = r * (dyg - xh * m)
        dh_ref[...] = dh
        dhb_ref[...] = dh.astype(BF16)

        @pl.when(pl.program_id(0) == 0)
        def _():
            dg_ref[...] = jnp.zeros_like(dg_ref)
            loss_ref[...] = jnp.zeros_like(loss_ref)

        dg_ref[...] += jnp.sum(dy * xh, axis=0, keepdims=True)
        loss_ref[...] += jnp.broadcast_to(part, loss_ref.shape)

    row = pl.BlockSpec((tr, D), lambda i: (i, 0))
    vec = pl.BlockSpec((1, D), lambda i: (0, 0))
    return pl.pallas_call(
        body, name="loss_head", grid=(T // tr,),
        in_specs=[row, vec, row],
        out_specs=[pl.BlockSpec((1, LANES), lambda i: (0, 0)), row, row, vec],
        out_shape=[jax.ShapeDtypeStruct((1, LANES), F32), jax.ShapeDtypeStruct((T, D), F32),
                   jax.ShapeDtypeStruct((T, D), BF16), jax.ShapeDtypeStruct((1, D), F32)],
        compiler_params=_cparams(("arbitrary",)),
    )(x, g, tgt)


def _mlstm_gates(proj, b_if_pad, D):
    T = proj.shape[0]
    tr = _pick(T, 512, CHUNK)

    def body(p_ref, b_ref, o_ref):
        z = p_ref[...] + b_ref[...]
        lane = lax.broadcasted_iota(jnp.int32, z.shape, 1)
        row = lax.broadcasted_iota(jnp.int32, z.shape, 0) % CHUNK
        lf = jnp.where((lane >= HEADS) & (lane < 2 * HEADS),
                       jnp.minimum(z, 0.0) - jnp.log(1.0 + jnp.exp(-jnp.abs(z))), 0.0)
        c = lf
        d = 1
        while d < CHUNK:
            c = c + jnp.where(row >= d, pltpu.roll(c, d, 0), 0.0)
            d *= 2
        o_ref[...] = jnp.where(lane < HEADS, z, 0.0) + lf + pltpu.roll(c, HEADS, 1)

    return pl.pallas_call(
        body, name="mlstm_gates", grid=(T // tr,),
        in_specs=[pl.BlockSpec((tr, IF_PAD), lambda i: (i, 3 * D // IF_PAD)),
                  pl.BlockSpec((1, IF_PAD), lambda i: (0, 0))],
        out_specs=pl.BlockSpec((tr, IF_PAD), lambda i: (i, 0)),
        out_shape=jax.ShapeDtypeStruct((T, IF_PAD), F32),
        compiler_params=_cparams(("parallel",)),
    )(proj, b_if_pad)


def _mlstm_chunk_common(qf, kf, vf, gc, gr, h, c_prev, n_prev, m_prev, causal):
    L = CHUNK
    qb = qf.astype(BF16)
    kb = kf.astype(BF16)
    vb = vf.astype(BF16)
    i_col = gc[:, h:h + 1]
    b_col = gc[:, 2 * HEADS + h:2 * HEADS + h + 1]
    i_row = gr[h:h + 1, :]
    b_row = gr[2 * HEADS + h:2 * HEADS + h + 1, :]
    dmat = jnp.where(causal, b_col - b_row + i_row, -jnp.inf)
    m_inter = b_col + m_prev
    m_t = jnp.maximum(m_inter, jnp.max(dmat, axis=-1, keepdims=True))
    w = jnp.exp(dmat - m_t)
    qk = _dot(qb, kb, "nt")
    s = qk * w
    sc = jnp.exp(m_inter - m_t)
    cb = c_prev.astype(BF16)
    num_inter = _dot(qb, cb)
    nb = n_prev.astype(BF16).astype(F32)
    qn = jnp.sum(qb.astype(F32) * nb, axis=-1, keepdims=True)
    num = _dot(s.astype(BF16), vb) + sc * num_inter
    den = jnp.sum(s, axis=-1, keepdims=True) + sc * qn
    e_m = jnp.exp(-m_t)
    nrm = jnp.maximum(jnp.abs(den), e_m)
    hh = num / nrm
    b_last = b_row[:, L - 1:L]
    g_row = b_last - b_row + i_row
    g_col = b_last - b_col + i_col
    m_new = jnp.maximum(b_last + m_prev, jnp.max(g_row, axis=-1, keepdims=True))
    wk = jnp.exp(g_col - m_new)
    decay = jnp.exp(b_last + m_prev - m_new)
    kw = kf * wk
    return dict(qb=qb, kb=kb, vb=vb, w=w, s=s, sc=sc, cb=cb, nb=nb, qn=qn, num_inter=num_inter,
                den=den, e_m=e_m, nrm=nrm, hh=hh, m_new=m_new, wk=wk, decay=decay, kw=kw)


def _mlstm_specs(T, D):
    qk = D // 2
    row = lambda w, j: pl.BlockSpec((CHUNK, w), lambda c, w=w, j=j: (c, j))
    return [row(qk, 0), row(qk, 1), row(D, 1), row(D, 2)]


def _mlstm_fwd(proj, gcol, grow, head_norm, carry=None):
    T = proj.shape[0]
    D = head_norm.shape[1]
    nc = T // CHUNK
    dk, dv = D // 2 // HEADS, D // HEADS
    kscale = dk ** -0.5
    L = CHUNK
    nci = len(carry.ins) if carry else 0
    nco = len(carry.out_shapes) if carry else 0

    def body(*refs):
        q_ref, k_ref, v_ref, o_ref, gc_ref, gr_ref, hn_ref = refs[:7]
        c_in = refs[7:7 + nci]
        hg_ref, cs_ref, ns_ref, ms_ref = refs[7 + nci:11 + nci]
        c_out = refs[11 + nci:11 + nci + nco]
        c_s, n_s, m_s = refs[11 + nci + nco:14 + nci + nco]
        c_scr = refs[14 + nci + nco:]

        @pl.when(pl.program_id(0) == 0)
        def _():
            c_s[...] = jnp.zeros_like(c_s)
            n_s[...] = jnp.zeros_like(n_s)
            m_s[...] = jnp.zeros_like(m_s)
            if carry:
                carry.start(c_in, c_out, c_scr)

        cs_ref[0] = c_s[...]
        ns_ref[0] = n_s[...]
        ms_ref[0] = m_s[...]
        causal = lax.broadcasted_iota(jnp.int32, (L, L), 1) <= lax.broadcasted_iota(jnp.int32, (L, L), 0)
        gc = gc_ref[...]
        gr = gr_ref[0]
        for h in range(HEADS):
            qf = q_ref[:, h * dk:(h + 1) * dk]
            kf = k_ref[:, h * dk:(h + 1) * dk] * kscale
            vf = v_ref[:, h * dv:(h + 1) * dv]
            m_prev = m_s[h:h + 1, 0:1]
            f = _mlstm_chunk_common(qf, kf, vf, gc, gr, h, c_s[h], n_s[h:h + 1, :], m_prev, causal)
            c_s[h] = f["decay"] * c_s[h] + _dot(f["kw"].astype(BF16), f["vb"], "tn")
            n_s[h:h + 1, :] = f["decay"] * n_s[h:h + 1, :] + jnp.sum(f["kw"], axis=0, keepdims=True)
            m_s[h:h + 1, :] = jnp.broadcast_to(f["m_new"], (1, LANES))
            hh = f["hh"]
            rs = lax.rsqrt(jnp.mean(hh * hh, axis=-1, keepdims=True) + EPS)
            sl = slice(h * dv, (h + 1) * dv)
            hg_ref[:, sl] = (hh * rs * hn_ref[:, sl] * _sigmoid(o_ref[:, sl])).astype(BF16)
        if carry:
            @pl.when(pl.program_id(0) == nc - 1)
            def _():
                carry.finish(c_in, c_out, c_scr)

    res = pl.pallas_call(
        body, name="mlstm_fwd", grid=(nc,),
        in_specs=_mlstm_specs(T, D) + [
            pl.BlockSpec((L, IF_PAD), lambda c: (c, 0)),
            pl.BlockSpec((1, 3 * HEADS, L), lambda c: (c, 0, 0)),
            pl.BlockSpec((1, D), lambda c: (0, 0)),
        ] + [_HBM] * nci,
        out_specs=[
            pl.BlockSpec((L, D), lambda c: (c, 0)),
            pl.BlockSpec((1, HEADS, dk, dv), lambda c: (c, 0, 0, 0)),
            pl.BlockSpec((1, HEADS, dk), lambda c: (c, 0, 0)),
            pl.BlockSpec((1, HEADS, LANES), lambda c: (c, 0, 0)),
        ] + [_HBM] * nco,
        out_shape=[
            jax.ShapeDtypeStruct((T, D), BF16),
            jax.ShapeDtypeStruct((nc, HEADS, dk, dv), F32),
            jax.ShapeDtypeStruct((nc, HEADS, dk), F32),
            jax.ShapeDtypeStruct((nc, HEADS, LANES), F32),
        ] + (list(carry.out_shapes) if carry else []),
        scratch_shapes=[pltpu.VMEM((HEADS, dk, dv), F32), pltpu.VMEM((HEADS, dk), F32),
                        pltpu.VMEM((HEADS, LANES), F32)] + (list(carry.scratch) if carry else []),
        compiler_params=_cparams(("arbitrary",)),
    )(proj, proj, proj, proj, gcol, grow, head_norm, *(carry.ins if carry else []))
    return res[:4], res[4:]


def _mlstm_bwd(proj, gcol, grow, head_norm, dhg, cs, ns, ms):
    T = proj.shape[0]
    D = head_norm.shape[1]
    nc = T // CHUNK
    dk, dv = D // 2 // HEADS, D // HEADS
    qkw = D // 2
    kscale = dk ** -0.5
    L = CHUNK
    pw = 3 * D + IF_PAD

    def body(q_ref, k_ref, v_ref, o_ref, gc_ref, gr_ref, hn_ref, dhg_ref, cs_ref, ns_ref, ms_ref,
             dp_ref, dgain_ref, dbif_ref, dc_s, dn_s):
        @pl.when(pl.program_id(0) == 0)
        def _():
            dc_s[...] = jnp.zeros_like(dc_s)
            dn_s[...] = jnp.zeros_like(dn_s)
            dgain_ref[...] = jnp.zeros_like(dgain_ref)
            dbif_ref[...] = jnp.zeros_like(dbif_ref)

        rowl = lax.broadcasted_iota(jnp.int32, (L, L), 0)
        coll = lax.broadcasted_iota(jnp.int32, (L, L), 1)
        causal = coll <= rowl
        eye = coll == rowl
        lane = lax.broadcasted_iota(jnp.int32, (L, IF_PAD), 1)
        rowg = lax.broadcasted_iota(jnp.int32, (L, IF_PAD), 0)
        gc = gc_ref[...]
        gr = gr_ref[0]
        dgate = jnp.zeros((L, IF_PAD), F32)
        for h in range(HEADS):
            qf = q_ref[:, h * dk:(h + 1) * dk]
            kf = k_ref[:, h * dk:(h + 1) * dk] * kscale
            vf = v_ref[:, h * dv:(h + 1) * dv]
            c_prev = cs_ref[0, h]
            n_prev = ns_ref[0, h:h + 1, :]
            m_prev = ms_ref[0, h:h + 1, 0:1]
            f = _mlstm_chunk_common(qf, kf, vf, gc, gr, h, c_prev, n_prev, m_prev, causal)
            qb, kb, vb, w, s, sc = f["qb"], f["kb"], f["vb"], f["w"], f["s"], f["sc"]
            hh, nrm, den = f["hh"], f["nrm"], f["den"]
            sl = slice(h * dv, (h + 1) * dv)
            dhg_v = dhg_ref[:, sl]
            sig = _sigmoid(o_ref[:, sl])
            gain = hn_ref[:, sl]
            rs = lax.rsqrt(jnp.mean(hh * hh, axis=-1, keepdims=True) + EPS)
            hn_v = hh * rs
            d_o = dhg_v * hn_v * gain * sig * (1.0 - sig)
            dgain_ref[:, sl] += jnp.sum(dhg_v * hn_v * sig, axis=0, keepdims=True)
            dhn = dhg_v * gain * sig
            dh = rs * (dhn - hn_v * jnp.mean(dhn * hn_v, axis=-1, keepdims=True))
            dnum = dh / nrm
            ddn = -jnp.sum(dh * hh, axis=-1, keepdims=True) / nrm
            dden = jnp.where(jnp.abs(den) > f["e_m"], jnp.where(den > 0.0, ddn, -ddn), 0.0)
            dnum_b = dnum.astype(BF16)
            d_s = _dot(dnum_b, vb, "nt") + dden
            dqk = (d_s * w).astype(BF16)
            p = d_s * s
            dcn = dc_s[h]
            dcb = dcn.astype(BF16)
            dnn = dn_s[h:h + 1, :]
            kwb = f["kw"].astype(BF16)
            d_v = _dot(s.astype(BF16), dnum_b, "tn") + _dot(kwb, dcb)
            dkw = _dot(vb, dcb, "nt") + dnn
            d_q = _dot(dqk, kb) + sc * (_dot(dnum_b, f["cb"], "nt") + dden * f["nb"])
            d_kf = _dot(dqk, qb, "tn") + f["wk"] * dkw
            u_col = f["wk"] * jnp.sum(kf * dkw, axis=-1, keepdims=True)
            z = f["decay"] * (jnp.sum(jnp.sum(c_prev * dcn, axis=-1, keepdims=True), axis=0, keepdims=True)
                              + jnp.sum(n_prev * dnn, axis=-1, keepdims=True))
            r_col = sc * (jnp.sum(dnum * f["num_inter"], axis=-1, keepdims=True) + dden * f["qn"])
            rowsum_p = jnp.sum(p, axis=-1, keepdims=True)
            colsum_p = jnp.sum(p, axis=0, keepdims=True)
            colsum_col = jnp.sum(jnp.where(eye, colsum_p, 0.0), axis=-1, keepdims=True)
            di_col = colsum_col + u_col
            db_col = rowsum_p + r_col - colsum_col - u_col
            db_last = jnp.sum(u_col, axis=0, keepdims=True) + z
            qs = (qb.astype(F32) * sc)
            dc_s[h] = f["decay"] * dcn + _dot(qs.astype(BF16), dnum_b, "tn")
            dn_s[h:h + 1, :] = f["decay"] * dnn + jnp.sum(qs * dden, axis=0, keepdims=True)
            dgate = jnp.where(lane == h, di_col, dgate)
            dgate = jnp.where(lane == 2 * HEADS + h,
                              db_col + jnp.where(rowg == L - 1, db_last, 0.0), dgate)
            dp_ref[:, h * dk:(h + 1) * dk] = d_q.astype(BF16)
            dp_ref[:, qkw + h * dk:qkw + (h + 1) * dk] = (d_kf * kscale).astype(BF16)
            dp_ref[:, D + h * dv:D + (h + 1) * dv] = d_v.astype(BF16)
            dp_ref[:, 2 * D + h * dv:2 * D + (h + 1) * dv] = d_o.astype(BF16)
        rc = jnp.where(lane >= 2 * HEADS, dgate, 0.0)
        d = 1
        while d < L:
            rc = rc + jnp.where(rowg < L - d, pltpu.roll(rc, L - d, 0), 0.0)
            d *= 2
        dlf = pltpu.roll(rc, IF_PAD - HEADS, 1)
        dif = jnp.where(lane < HEADS, dgate, 0.0) + jnp.where(
            (lane >= HEADS) & (lane < 2 * HEADS), dlf * (1.0 - jnp.exp(gc)), 0.0)
        dp_ref[:, 3 * D:3 * D + IF_PAD] = dif.astype(BF16)
        dbif_ref[...] += jnp.sum(dif, axis=0, keepdims=True)

    rev = lambda c: nc - 1 - c
    specs = [pl.BlockSpec(s.block_shape, (lambda c, f=s.index_map: f(rev(c)))) for s in _mlstm_specs(T, D)]
    return pl.pallas_call(
        body, name="mlstm_bwd", grid=(nc,),
        in_specs=specs + [
            pl.BlockSpec((L, IF_PAD), lambda c: (rev(c), 0)),
            pl.BlockSpec((1, 3 * HEADS, L), lambda c: (rev(c), 0, 0)),
            pl.BlockSpec((1, D), lambda c: (0, 0)),
            pl.BlockSpec((L, D), lambda c: (rev(c), 0)),
            pl.BlockSpec((1, HEADS, dk, dv), lambda c: (rev(c), 0, 0, 0)),
            pl.BlockSpec((1, HEADS, dk), lambda c: (rev(c), 0, 0)),
            pl.BlockSpec((1, HEADS, LANES), lambda c: (rev(c), 0, 0)),
        ],
        out_specs=[
            pl.BlockSpec((L, pw), lambda c: (rev(c), 0)),
            pl.BlockSpec((1, D), lambda c: (0, 0)),
            pl.BlockSpec((1, IF_PAD), lambda c: (0, 0)),
        ],
        out_shape=[
            jax.ShapeDtypeStruct((T, pw), BF16),
            jax.ShapeDtypeStruct((1, D), F32),
            jax.ShapeDtypeStruct((1, IF_PAD), F32),
        ],
        scratch_shapes=[pltpu.VMEM((HEADS, dk, dv), F32), pltpu.VMEM((HEADS, dk), F32)],
        compiler_params=_cparams(("arbitrary",)),
    )(proj, proj, proj, proj, gcol, grow, head_norm, dhg, cs, ns, ms)


_GELU_C = math.sqrt(2.0 / math.pi)


def _log_sigmoid(x):
    return jnp.minimum(x, 0.0) - jnp.log(1.0 + jnp.exp(-jnp.abs(x)))


def _rglru_recompute(rec_ref, halo_ref, cw_ref, cb_ref, gw_ref, gb_ref, ap_ref, first, W, Lt):
    x = rec_ref[...]
    halo = jnp.where(first, 0.0, halo_ref[...])
    xe = jnp.concatenate([halo, x], axis=0)
    xs = [pltpu.roll(xe, d, 0)[8:8 + Lt] for d in (3, 2, 1)] + [x]
    cw = cw_ref[...]
    rec_c = cb_ref[...] + cw[0:1] * xs[0] + cw[1:2] * xs[1] + cw[2:3] * xs[2] + cw[3:4] * xs[3]
    gates = _dot(rec_c.astype(BF16), gw_ref[0]) + gb_ref[0]
    r = _sigmoid(gates[:, :W])
    ig = _sigmoid(gates[:, W:])
    lsa = _log_sigmoid(ap_ref[...])
    log_a = R_C * r * lsa
    a = jnp.exp(log_a)
    mult = jnp.sqrt(1.0 - jnp.exp(2.0 * log_a))
    return dict(xs=xs, cw=cw, rec_c=rec_c, r=r, ig=ig, lsa=lsa, a=a, mult=mult)


def _rglru_specs(T, D, Lt, tmap):
    W = D // RBLOCKS
    hb = Lt // 8
    return [
        pl.BlockSpec((Lt, W), lambda g, i: (tmap(i), g)),
        pl.BlockSpec((Lt, W), lambda g, i: (tmap(i), RBLOCKS + g)),
        pl.BlockSpec((8, W), lambda g, i: (jnp.maximum(tmap(i) * hb - 1, 0), RBLOCKS + g)),
        pl.BlockSpec((CONV_W, W), lambda g, i: (0, g)),
        pl.BlockSpec((1, W), lambda g, i: (0, g)),
        pl.BlockSpec((1, W, 2 * W), lambda g, i: (g, 0, 0)),
        pl.BlockSpec((1, 1, 2 * W), lambda g, i: (g, 0, 0)),
        pl.BlockSpec((1, W), lambda g, i: (0, g)),
    ]


def _rglru_fwd(pr, conv_w, conv_b, gate_w, gate_b, a_param, carry_ex=None):
    T = pr.shape[0]
    D = pr.shape[1] // 2
    W = D // RBLOCKS
    Lt = _pick(T, 256, 8)
    nt = T // Lt
    nci = len(carry_ex.ins) if carry_ex else 0
    nco = len(carry_ex.out_shapes) if carry_ex else 0

    def body(*refs):
        gate_ref, rec_ref, halo_ref, cw_ref, cb_ref, gw_ref, gb_ref, ap_ref = refs[:8]
        c_in = refs[8:8 + nci]
        y_ref, hs_ref = refs[8 + nci:10 + nci]
        c_out = refs[10 + nci:10 + nci + nco]
        carry = refs[10 + nci + nco]
        c_scr = refs[11 + nci + nco:]
        i = pl.program_id(1)
        if carry_ex:
            @pl.when((pl.program_id(0) == 0) & (i == 0))
            def _():
                carry_ex.start(c_in, c_out, c_scr)

        @pl.when(i == 0)
        def _():
            carry[...] = jnp.zeros_like(carry)

        f = _rglru_recompute(rec_ref, halo_ref, cw_ref, cb_ref, gw_ref, gb_ref, ap_ref, i == 0, W, Lt)
        row = lax.broadcasted_iota(jnp.int32, (Lt, W), 0)
        a_c = f["a"]
        u_c = f["mult"] * (f["ig"] * f["rec_c"])
        d = 1
        while d < Lt:
            msk = row >= d
            u_c = jnp.where(msk, a_c * pltpu.roll(u_c, d, 0) + u_c, u_c)
            a_c = jnp.where(msk, a_c * pltpu.roll(a_c, d, 0), a_c)
            d *= 2
        h = u_c + a_c * carry[0:1, :]
        carry[0:1, :] = h[Lt - 1:Lt, :]
        hs_ref[...] = h
        gb = gate_ref[...]
        t = jnp.tanh(_GELU_C * (gb + 0.044715 * gb * gb * gb))
        y_ref[...] = (0.5 * gb * (1.0 + t) * h).astype(BF16)
        if carry_ex:
            @pl.when((pl.program_id(0) == RBLOCKS - 1) & (i == nt - 1))
            def _():
                carry_ex.finish(c_in, c_out, c_scr)

    blk = pl.BlockSpec((Lt, W), lambda g, i: (i, g))
    res = pl.pallas_call(
        body, name="rglru_fwd", grid=(RBLOCKS, nt),
        in_specs=_rglru_specs(T, D, Lt, lambda i: i) + [_HBM] * nci,
        out_specs=[blk, blk] + [_HBM] * nco,
        out_shape=[jax.ShapeDtypeStruct((T, D), BF16), jax.ShapeDtypeStruct((T, D), F32)]
        + (list(carry_ex.out_shapes) if carry_ex else []),
        scratch_shapes=[pltpu.VMEM((8, W), F32)] + (list(carry_ex.scratch) if carry_ex else []),
        compiler_params=_cparams(("arbitrary", "arbitrary") if carry_ex else ("parallel", "arbitrary")),
    )(pr, pr, pr, conv_w, conv_b, gate_w, gate_b, a_param, *(carry_ex.ins if carry_ex else []))
    return res[:2], res[2:]


def _rglru_bwd(pr, hs, dy, conv_w, conv_b, gate_w, gate_b, a_param):
    T = pr.shape[0]
    D = pr.shape[1] // 2
    W = D // RBLOCKS
    Lt = _pick(T, 256, 8)
    nt = T // Lt
    hb = Lt // 8
    tmap = lambda i: nt - 1 - i

    def body(gate_ref, rec_ref, halo_ref, cw_ref, cb_ref, gw_ref, gb_ref, ap_ref, hs_ref, hh_ref, dy_ref,
             dgate_ref, drec_ref, dcw_ref, dcb_ref, dgw_ref, dgb_ref, dap_ref, lam_s, drc_s):
        i = pl.program_id(1)
        ti = nt - 1 - i

        @pl.when(i == 0)
        def _():
            lam_s[...] = jnp.zeros_like(lam_s)
            drc_s[...] = jnp.zeros_like(drc_s)
            dcw_ref[...] = jnp.zeros_like(dcw_ref)
            dcb_ref[...] = jnp.zeros_like(dcb_ref)
            dgw_ref[...] = jnp.zeros_like(dgw_ref)
            dgb_ref[...] = jnp.zeros_like(dgb_ref)
            dap_ref[...] = jnp.zeros_like(dap_ref)

        f = _rglru_recompute(rec_ref, halo_ref, cw_ref, cb_ref, gw_ref, gb_ref, ap_ref, ti == 0, W, Lt)
        a, mult, ig, r, rec_c, lsa, xs, cw = (f[k] for k in ("a", "mult", "ig", "r", "rec_c", "lsa", "xs", "cw"))
        row = lax.broadcasted_iota(jnp.int32, (Lt, W), 0)
        gb = gate_ref[...]
        t = jnp.tanh(_GELU_C * (gb + 0.044715 * gb * gb * gb))
        gel = 0.5 * gb * (1.0 + t)
        dgel = 0.5 * (1.0 + t) + 0.5 * gb * (1.0 - t * t) * _GELU_C * (1.0 + 3.0 * 0.044715 * gb * gb)
        h = hs_ref[...]
        h_first = jnp.where(ti == 0, 0.0, hh_ref[7:8, :])
        h_prev = jnp.where(row == 0, h_first, pltpu.roll(h, 1, 0))
        dy_v = dy_ref[...]
        d_gb = dy_v * h * dgel
        c_c = jnp.where(row < Lt - 1, pltpu.roll(a, Lt - 1, 0), 0.0)
        g_c = dy_v * gel + jnp.where(row == Lt - 1, lam_s[0:1, :], 0.0)
        d = 1
        while d < Lt:
            msk = row < Lt - d
            g_c = jnp.where(msk, c_c * pltpu.roll(g_c, Lt - d, 0) + g_c, g_c)
            c_c = jnp.where(msk, c_c * pltpu.roll(c_c, Lt - d, 0), c_c)
            d *= 2
        lam = g_c
        lam_s[0:1, :] = (a * lam)[0:1, :]
        d_mult = lam * ig * rec_c
        d_loga = lam * h_prev * a - d_mult * (a * a) / mult
        d_ig = lam * mult * rec_c
        d_rec = lam * mult * ig
        d_r = d_loga * (R_C * lsa)
        dap_ref[...] += jnp.sum(d_loga * (R_C * r), axis=0, keepdims=True) * (1.0 - jnp.exp(lsa))
        dgates = jnp.concatenate([d_r * r * (1.0 - r), d_ig * ig * (1.0 - ig)], axis=1)
        dgb_ref[0] += jnp.sum(dgates, axis=0, keepdims=True)
        dgates_b = dgates.astype(BF16)
        dgw_ref[0] += _dot(rec_c.astype(BF16), dgates_b, "tn")
        d_rec = d_rec + _dot(dgates_b, gw_ref[0], "nt")
        dcb_ref[...] += jnp.sum(d_rec, axis=0, keepdims=True)
        dcw_ref[...] += jnp.concatenate(
            [jnp.sum(d_rec * xs[j], axis=0, keepdims=True) for j in range(CONV_W)], axis=0)
        ext = jnp.concatenate([d_rec, drc_s[...]], axis=0)
        up = lambda s: pltpu.roll(ext, Lt + 8 - s, 0)[:Lt]
        d_x = cw[3:4] * d_rec + cw[2:3] * up(1) + cw[1:2] * up(2) + cw[0:1] * up(3)
        drc_s[...] = d_rec[0:8, :]
        dgate_ref[...] = d_gb.astype(BF16)
        drec_ref[...] = d_x.astype(BF16)

    blk = pl.BlockSpec((Lt, W), lambda g, i: (tmap(i), g))
    return pl.pallas_call(
        body, name="rglru_bwd", grid=(RBLOCKS, nt),
        in_specs=_rglru_specs(T, D, Lt, tmap) + [
            blk,
            pl.BlockSpec((8, W), lambda g, i: (jnp.maximum(tmap(i) * hb - 1, 0), g)),
            blk,
        ],
        out_specs=[
            blk,
            blk,
            pl.BlockSpec((CONV_W, W), lambda g, i: (0, g)),
            pl.BlockSpec((1, W), lambda g, i: (0, g)),
            pl.BlockSpec((1, W, 2 * W), lambda g, i: (g, 0, 0)),
            pl.BlockSpec((1, 1, 2 * W), lambda g, i: (g, 0, 0)),
            pl.BlockSpec((1, W), lambda g, i: (0, g)),
        ],
        out_shape=[
            jax.ShapeDtypeStruct((T, D), BF16),
            jax.ShapeDtypeStruct((T, D), BF16),
            jax.ShapeDtypeStruct((CONV_W, D), F32),
            jax.ShapeDtypeStruct((1, D), F32),
            jax.ShapeDtypeStruct((RBLOCKS, W, 2 * W), F32),
            jax.ShapeDtypeStruct((RBLOCKS, 1, 2 * W), F32),
            jax.ShapeDtypeStruct((1, D), F32),
        ],
        scratch_shapes=[pltpu.VMEM((8, W), F32), pltpu.VMEM((8, W), F32)],
        compiler_params=_cparams(("parallel", "arbitrary")),
    )(pr, pr, pr, conv_w, conv_b, gate_w, gate_b, a_param, hs, hs, dy)


def _adamw(name, w, gs, m, v):
    L, R, C = w.shape
    if R % 8 == 0:
        tr, tc = _pick(R, max(8, (1 << 19) // C), 8), C
    else:
        tr, tc = R, _pick(C, max(LANES, (1 << 19) // R), LANES)

    def body(*refs):
        w_ref, m_ref, v_ref = refs[:3]
        g_refs = refs[3:3 + L]
        go_ref, d_ref, nm_ref, nv_ref = refs[3 + L:]
        l = pl.program_id(0)
        gv = g_refs[0][...]
        for q in range(1, L):
            gv = jnp.where(l == q, g_refs[q][...], gv)
        mv = ADAM_B1 * m_ref[0] + (1.0 - ADAM_B1) * gv
        vv = ADAM_B2 * v_ref[0] + (1.0 - ADAM_B2) * (gv * gv)
        m_hat = mv / (1.0 - ADAM_B1 ** ADAM_STEP)
        v_hat = vv / (1.0 - ADAM_B2 ** ADAM_STEP)
        go_ref[0] = gv
        d_ref[0] = -ADAM_LR * (m_hat / (jnp.sqrt(v_hat) + ADAM_EPS) + ADAM_WD * w_ref[0])
        nm_ref[0] = mv
        nv_ref[0] = vv

    blk = pl.BlockSpec((1, tr, tc), lambda l, i, j: (l, i, j))
    g_specs = [pl.BlockSpec((tr, tc), lambda l, i, j, q=q: (jnp.where(l == q, i, 0), jnp.where(l == q, j, 0)))
               for q in range(L)]
    sds = jax.ShapeDtypeStruct((L, R, C), F32)
    return pl.pallas_call(
        body, name=name, grid=(L, R // tr, C // tc),
        in_specs=[blk] * 3 + g_specs, out_specs=[blk] * 4, out_shape=[sds] * 4,
        compiler_params=_cparams(("parallel", "parallel", "parallel")),
    )(w, m, v, *gs)


def _step(x, tgt, p, weights, gather_plan, scatter_plan):
    T, D = x.shape
    pw = 3 * D + IF_PAD
    nc = T // CHUNK
    tm = _pick(T, 1024, 128)
    tkt = _pick(T, 2048, 128)
    td = _pick(D, 1024, 128)
    vec = lambda a, l: a[l:l + 1]
    w = dict(weights)
    g, landed = {}, {}

    def mm(name, a_op, b_op, mode, M, N, K, out_defs, **kw):
        carry = None
        if name in gather_plan:
            carry = gather_plan[name]
        elif name in scatter_plan:
            carry = scatter_plan[name](g)
        outs, extra = _mm(name, a_op, b_op, mode, M, N, K, out_defs, carry=carry, **kw)
        if name in gather_plan:
            for n, a in zip(carry.names, extra):
                w[n] = carry.post[n](a) if n in carry.post else a
        elif carry is not None:
            landed.update(zip(carry.names, extra))
        return outs

    def ffn_fwd(l, h_in):
        dff = w[f"ffn{l}_w_in"].shape[1] // 2
        hn, r = _rms_fwd(f"ffn{l}_norm", h_in, vec(p["norm_ffn"], l))
        tn = _pick(dff, 512, 128)
        gt, ut, act = mm(f"ffn{l}_in", _a_plain(hn, "nn", tm, D), _b_pair(w[f"ffn{l}_w_in"], tn, D), "nn",
                         T, dff, D, [(BF16, dff)] * 3, tm=tm, tn=tn, tk=D, epi=_epi_swiglu)
        tk = _pick(dff, 2816, 128)
        (h_out,) = mm(f"ffn{l}_out", _a_plain(act, "nn", tm, tk), _b_plain(w[f"ffn{l}_w_out"], "nn", td, tk), "nn",
                      T, D, dff, [(F32, D)], tm=tm, tn=td, tk=tk, epi=_epi_resid, extras=[(h_in, td, 0)])
        return h_out, (hn, r, gt, ut, act)

    def ffn_bwd(l, h_in, saved, dh, dhb):
        hn, r, gt, ut, act = saved
        dff = w[f"ffn{l}_w_out"].shape[0]
        tn = _pick(dff, 512, 128)
        dg, du = mm(f"ffn{l}_dact", _a_plain(dhb, "nt", tm, D), _b_plain(w[f"ffn{l}_w_out"], "nt", tn, D), "nt",
                    T, dff, D, [(BF16, dff), (BF16, dff)], tm=tm, tn=tn, tk=D, epi=_epi_swiglu_bwd,
                    extras=[(gt, tn, 0), (ut, tn, 0)])
        tmf, tkf = _pick(dff, 2816, 128), _pick(T, 1024, 128)
        (g[f"ffn{l}_w_out"],) = mm(f"ffn{l}_dwout", _a_plain(act, "tn", tmf, tkf), _b_plain(dhb, "tn", td, tkf), "tn",
                                   dff, D, T, [(BF16, D)], tm=tmf, tn=td, tk=tkf)
        tn2, tk1 = _pick(dff, 2816, 128), _pick(T, 512, 128)
        (g[f"ffn{l}_w_in"],) = mm(f"ffn{l}_dwin", _a_plain(hn, "tn", td, tk1), _b_split_n(dg, du, tn2, tk1), "tn",
                                  D, 2 * dff, T, [(BF16, 2 * dff)], tm=td, tn=tn2, tk=tk1)
        tk2 = _pick(dff, 2048, 128)
        (dhn,) = mm(f"ffn{l}_dx", _a_split_k(dg, du, tm, tk2), _b_plain(w[f"ffn{l}_w_in"], "nt", td, tk2), "nt",
                    T, D, 2 * dff, [(F32, D)], tm=tm, tn=td, tk=tk2)
        dh, dhb, dgn = _rms_bwd(f"ffn{l}_dnorm", h_in, r, dhn, vec(p["norm_ffn"], l), dh)
        return dh, dhb, dgn

    h0 = x
    hn0, r0 = _rms_fwd("mix0_norm", h0, vec(p["norm_mix"], 0))
    tnp = _pick(pw, 1024, 128)
    (proj,) = mm("m_proj", _a_plain(hn0, "nn", tm, D), _b_plain(w["m_w_in"], "nn", tnp, D), "nn",
                 T, pw, D, [(F32, pw)], tm=tm, tn=tnp, tk=D)
    gcol = _mlstm_gates(proj, p["m_b_if"], D)
    grow = gcol[:, :3 * HEADS].reshape(nc, CHUNK, 3 * HEADS).transpose(0, 2, 1)
    carry = gather_plan.get("mlstm_fwd")
    (hg, cs, ns, ms), extra = _mlstm_fwd(proj, gcol, grow, p["m_head_norm"], carry)
    if carry is not None:
        for n, a in zip(carry.names, extra):
            w[n] = carry.post[n](a) if n in carry.post else a
    (h1,) = mm("m_out", _a_plain(hg, "nn", tm, D), _b_plain(w["m_w_out"], "nn", td, D), "nn",
               T, D, D, [(F32, D)], tm=tm, tn=td, tk=D, epi=_epi_resid, extras=[(h0, td, 0)])
    h2, ffn0 = ffn_fwd(0, h1)
    hn2, r2 = _rms_fwd("mix1_norm", h2, vec(p["norm_mix"], 1))
    (pr,) = mm("r_proj", _a_plain(hn2, "nn", tm, D), _b_plain(w["r_w_in"], "nn", td, D), "nn",
               T, 2 * D, D, [(F32, 2 * D)], tm=tm, tn=td, tk=D)
    carry = gather_plan.get("rglru_fwd")
    (y, hs), extra = _rglru_fwd(pr, p["conv_w"], p["conv_b"], w["gate_w"], p["gate_b"], p["a_param"], carry)
    if carry is not None:
        for n, a in zip(carry.names, extra):
            w[n] = carry.post[n](a) if n in carry.post else a
    (h3,) = mm("r_out", _a_plain(y, "nn", tm, D), _b_plain(w["r_w_out"], "nn", td, D), "nn",
               T, D, D, [(F32, D)], tm=tm, tn=td, tk=D, epi=_epi_resid, extras=[(h2, td, 0)])
    h4, ffn1 = ffn_fwd(1, h3)
    loss, dh, dhb, g["norm_final"] = _loss_head(h4, p["norm_final"], tgt)

    dh, dhb, dnf1 = ffn_bwd(1, h3, ffn1, dh, dhb)
    (dy,) = mm("r_dy", _a_plain(dhb, "nt", tm, D), _b_plain(w["r_w_out"], "nt", td, D), "nt",
               T, D, D, [(F32, D)], tm=tm, tn=td, tk=D)
    (g["r_w_out"],) = mm("r_dwout", _a_plain(y, "tn", td, tkt), _b_plain(dhb, "tn", td, tkt), "tn",
                         D, D, T, [(BF16, D)], tm=td, tn=td, tk=tkt)
    dgate, drec, g["conv_w"], g["conv_b"], dgw, g["gate_b"], g["a_param"] = _rglru_bwd(
        pr, hs, dy, p["conv_w"], p["conv_b"], w["gate_w"], p["gate_b"], p["a_param"])
    g["gate_w"] = dgw.astype(BF16)
    (g["r_w_in"],) = mm("r_dwin", _a_plain(hn2, "tn", td, tkt), _b_split_n(dgate, drec, td, tkt), "tn",
                        D, 2 * D, T, [(BF16, 2 * D)], tm=td, tn=td, tk=tkt)
    (dhn2,) = mm("r_dx", _a_split_k(dgate, drec, tm, td), _b_plain(w["r_w_in"], "nt", td, td), "nt",
                 T, D, 2 * D, [(F32, D)], tm=tm, tn=td, tk=td)
    dh, dhb, dnm1 = _rms_bwd("mix1_dnorm", h2, r2, dhn2, vec(p["norm_mix"], 1), dh)
    dh, dhb, dnf0 = ffn_bwd(0, h1, ffn0, dh, dhb)
    (dhg,) = mm("m_dhg", _a_plain(dhb, "nt", tm, D), _b_plain(w["m_w_out"], "nt", td, D), "nt",
                T, D, D, [(F32, D)], tm=tm, tn=td, tk=D)
    (g["m_w_out"],) = mm("m_dwout", _a_plain(hg, "tn", td, tkt), _b_plain(dhb, "tn", td, tkt), "tn",
                         D, D, T, [(BF16, D)], tm=td, tn=td, tk=tkt)
    dproj, g["m_head_norm"], g["m_b_if"] = _mlstm_bwd(proj, gcol, grow, p["m_head_norm"], dhg, cs, ns, ms)
    (g["m_w_in"],) = mm("m_dwin", _a_plain(hn0, "tn", td, tkt), _b_plain(dproj, "tn", tnp, tkt), "tn",
                        D, pw, T, [(BF16, pw)], tm=td, tn=tnp, tk=tkt)
    (dhn0,) = mm("m_dx", _a_plain(dproj, "nt", tm, tnp), _b_plain(w["m_w_in"], "nt", td, tnp), "nt",
                 T, D, pw, [(F32, D)], tm=tm, tn=td, tk=tnp)
    grad_x, _, dnm0 = _rms_bwd("mix0_dnorm", h0, r0, dhn0, vec(p["norm_mix"], 0), dh)
    g["norm_mix"] = jnp.concatenate([dnm0, dnm1], axis=0)
    g["norm_ffn"] = jnp.concatenate([dnf0, dnf1], axis=0)
    return loss, grad_x, g, landed


_MESH = pl.DeviceIdType.MESH
_HBM = pl.BlockSpec(memory_space=pltpu.HBM)
N_CHIPS = 4


def _place():
    x, y, c = lax.axis_index("x"), lax.axis_index("y"), lax.axis_index("c")
    chips = [(1 - x, y), (x, 1 - y), (1 - x, 1 - y)]
    return x, y, c, 2 * x + y, chips, [2 * px + py for px, py in chips]


class _Piece:
    def __init__(self, kind, rows, cols):
        self.kind, self.rows, self.cols, self.hr = kind, rows, cols, rows // 2

    def full_shape(self):
        return {"col": (self.rows, N_CHIPS * self.cols), "row": (N_CHIPS * self.rows, self.cols),
                "stack": (N_CHIPS, self.rows, self.cols)}[self.kind]

    def region(self, ref, s, h):
        rows = pl.ds(h * self.hr, self.hr)
        if self.kind == "col":
            return ref.at[rows, pl.ds(pl.multiple_of(s * self.cols, LANES), self.cols)]
        if self.kind == "row":
            return ref.at[pl.ds(pl.multiple_of(s * self.rows + h * self.hr, 16), self.hr), :]
        return ref.at[s, rows, :]

    def shard_half(self, ref, h):
        return ref.at[pl.ds(h * self.hr, self.hr), :]


class _Gather:
    def __init__(self, names, shards, pieces, post=None):
        n = len(names)
        self.names, self.ins, self.pieces, self.post = list(names), list(shards), list(pieces), post or {}
        self.out_shapes = [jax.ShapeDtypeStruct(p.full_shape(), s.dtype) for p, s in zip(pieces, shards)]
        self.scratch = [pltpu.SemaphoreType.DMA((n, 2))] + [pltpu.SemaphoreType.DMA((n, 3))] * 4

    def _copies(self, ins, outs, sems):
        loc, snd, rcv, fsnd, frcv = sems
        x, y, c, j, chips, cj = _place()

        def local(w, h):
            return pltpu.make_async_copy(self.pieces[w].shard_half(ins[w], h), self.pieces[w].region(outs[w], j, h),
                                         loc.at[w, h])

        def ici_send(w, k):
            return pltpu.make_async_remote_copy(
                src_ref=self.pieces[w].shard_half(ins[w], c), dst_ref=self.pieces[w].region(outs[w], j, c),
                send_sem=snd.at[w, k], recv_sem=rcv.at[w, k], device_id=(*chips[k], c), device_id_type=_MESH)

        def ici_recv(w, k):
            region = self.pieces[w].region(outs[w], cj[k], c)
            return pltpu.make_async_remote_copy(
                src_ref=region, dst_ref=region, send_sem=snd.at[w, k], recv_sem=rcv.at[w, k],
                device_id=(*chips[k], c), device_id_type=_MESH)

        def d2d(w, k, h):
            region = self.pieces[w].region(outs[w], cj[k], h)
            return pltpu.make_async_remote_copy(
                src_ref=region, dst_ref=region, send_sem=fsnd.at[w, k], recv_sem=frcv.at[w, k],
                device_id=(x, y, 1 - c), device_id_type=_MESH)

        return c, local, ici_send, ici_recv, d2d

    def start(self, ins, outs, sems):
        _, local, ici_send, _, _ = self._copies(ins, outs, sems)
        for w in range(len(self.names)):
            for k in range(3):
                ici_send(w, k).start()
        for w in range(len(self.names)):
            for h in range(2):
                local(w, h).start()

    def finish(self, ins, outs, sems):
        c, local, ici_send, ici_recv, d2d = self._copies(ins, outs, sems)
        n = len(self.names)
        for w in range(n):
            for k in range(3):
                ici_recv(w, k).wait_recv()
                d2d(w, k, c).start()
        for w in range(n):
            for k in range(3):
                d2d(w, k, 1 - c).wait_recv()
        for w in range(n):
            for k in range(3):
                ici_send(w, k).wait_send()
                d2d(w, k, c).wait_send()
            for h in range(2):
                local(w, h).wait()


class _Scatter:
    def __init__(self, names, grads, pieces):
        n = len(names)
        self.names, self.ins, self.pieces = list(names), list(grads), list(pieces)
        self.out_shapes = [jax.ShapeDtypeStruct((7, p.hr, p.cols), g.dtype) for p, g in zip(pieces, grads)]
        self.scratch = [pltpu.SemaphoreType.DMA((n, 7))] * 2

    def _copy(self, ins, outs, sems, w, k):
        x, y, c = lax.axis_index("x"), lax.axis_index("y"), lax.axis_index("c")
        px, py, pc = ((1 - x) if k & 4 else x, (1 - y) if k & 2 else y, (1 - c) if k & 1 else c)
        return pltpu.make_async_remote_copy(
            src_ref=self.pieces[w].region(ins[w], 2 * px + py, pc), dst_ref=outs[w].at[k - 1],
            send_sem=sems[0].at[w, k - 1], recv_sem=sems[1].at[w, k - 1], device_id=(px, py, pc),
            device_id_type=_MESH)

    def start(self, ins, outs, sems):
        for w in range(len(self.names)):
            for k in range(1, 8):
                self._copy(ins, outs, sems, w, k).start()

    def finish(self, ins, outs, sems):
        for w in range(len(self.names)):
            for k in range(1, 8):
                self._copy(ins, outs, sems, w, k).wait()


def _run_exchange(name, ex):
    ni, no = len(ex.ins), len(ex.out_shapes)

    def body(*refs):
        ins, outs, sems = refs[:ni], refs[ni:ni + no], refs[ni + no:]
        ex.start(ins, outs, sems)
        ex.finish(ins, outs, sems)

    return pl.pallas_call(
        body, name=name, in_specs=[_HBM] * ni, out_specs=[_HBM] * no, out_shape=list(ex.out_shapes),
        scratch_shapes=list(ex.scratch),
    )(*ex.ins)


def _rs_sum(name, piece, grad, landed, jc_idx):
    hr, C = piece.hr, piece.cols
    br = _pick(hr, max(16, (1 << 18) // C), 16)
    nb = hr // br
    if piece.kind == "col":
        g_spec = pl.BlockSpec((br, C), lambda i, jc: (jc[1] * nb + i, jc[0]))
    elif piece.kind == "row":
        g_spec = pl.BlockSpec((br, C), lambda i, jc: ((jc[0] * 2 + jc[1]) * nb + i, 0))
    else:
        g_spec = pl.BlockSpec((None, br, C), lambda i, jc: (jc[0], jc[1] * nb + i, 0))

    def body(jc_ref, g_ref, l_ref, o_ref):
        acc = g_ref[...].astype(F32)
        for k in range(7):
            acc = acc + l_ref[k].astype(F32)
        o_ref[0] = acc

    return pl.pallas_call(
        body, name=name,
        grid_spec=pltpu.PrefetchScalarGridSpec(
            num_scalar_prefetch=1, grid=(nb,),
            in_specs=[g_spec, pl.BlockSpec((7, br, C), lambda i, jc: (0, i, 0))],
            out_specs=pl.BlockSpec((1, br, C), lambda i, jc: (jc[1], i, 0))),
        out_shape=jax.ShapeDtypeStruct((2, hr, C), F32),
        compiler_params=_cparams(("parallel",)),
    )(jc_idx, grad, landed)


def _rs_share_halves(rs):
    n = len(rs)

    def body(*refs):
        outs = refs[n:2 * n]
        send_sems, recv_sems = refs[2 * n:]
        x, y, c, _, _, _ = _place()
        cps = []
        for w in range(n):
            cps.append(pltpu.make_async_remote_copy(
                src_ref=outs[w].at[c], dst_ref=outs[w].at[c], send_sem=send_sems.at[w],
                recv_sem=recv_sems.at[w], device_id=(x, y, 1 - c), device_id_type=_MESH))
            cps[-1].start()
        for w in range(n):
            cps[w].wait_send()
            pltpu.make_async_remote_copy(
                src_ref=outs[w].at[1 - c], dst_ref=outs[w].at[1 - c], send_sem=send_sems.at[w],
                recv_sem=recv_sems.at[w], device_id=(x, y, 1 - c), device_id_type=_MESH).wait_recv()

    return pl.pallas_call(
        body, name="rs_share_halves",
        in_specs=[_HBM] * n, out_specs=[_HBM] * n,
        out_shape=[jax.ShapeDtypeStruct(a.shape, a.dtype) for a in rs],
        input_output_aliases={w: w for w in range(n)},
        scratch_shapes=[pltpu.SemaphoreType.DMA((n,))] * 2,
    )(*rs)


def _allreduce_small(v):
    R = v.shape[0]

    def body(v_ref, o_ref, slots, send_sems, recv_sems):
        x, y, c = lax.axis_index("x"), lax.axis_index("y"), lax.axis_index("c")
        me = 4 * x + 2 * y + c
        cps = []
        for k in range(1, 8):
            peer = ((1 - x) if k & 4 else x, (1 - y) if k & 2 else y, (1 - c) if k & 1 else c)
            cps.append(pltpu.make_async_remote_copy(
                src_ref=v_ref, dst_ref=slots.at[k - 1], send_sem=send_sems.at[k - 1],
                recv_sem=recv_sems.at[k - 1], device_id=peer, device_id_type=_MESH))
            cps[-1].start()
        for cp in cps:
            cp.wait()
        acc = jnp.zeros((R, LANES), F32)
        for d in range(8):
            k = ((d // 4) ^ x) * 4 + (((d // 2) % 2) ^ y) * 2 + ((d % 2) ^ c)
            other = slots[jnp.maximum(k - 1, 0)]
            acc = acc + jnp.where(d == me, v_ref[...], other)
        o_ref[...] = acc

    vm = pl.BlockSpec(memory_space=pltpu.VMEM)
    return pl.pallas_call(
        body, name="allreduce_small",
        in_specs=[vm], out_specs=vm,
        out_shape=jax.ShapeDtypeStruct((R, LANES), F32),
        scratch_shapes=[pltpu.VMEM((7, R, LANES), F32), pltpu.SemaphoreType.DMA((7,)), pltpu.SemaphoreType.DMA((7,))],
    )(v)


def _rows(a):
    flat = a.reshape(-1)
    return jnp.pad(flat, (0, (-flat.shape[0]) % LANES)).reshape(-1, LANES)


def _pack_rows(parts, mult=8):
    v = jnp.concatenate([_rows(a) for a in parts], axis=0)
    return jnp.pad(v, ((0, (-v.shape[0]) % mult), (0, 0)))


def _unpack_rows(v, shapes):
    out, r = [], 0
    for s in shapes:
        size = math.prod(s)
        nr = -(-size // LANES)
        out.append(v[r:r + nr].reshape(-1)[:size].reshape(s))
        r += nr
    return out


def kernel(x, norm_mix, norm_ffn, norm_final, m_w_in, m_b_if, m_head_norm, m_w_out, r_w_in, r_conv_w, r_conv_b, r_gate_w, r_gate_b, r_a_param, r_w_out, ffn_w_in, ffn_w_out, loss_target, m_norm_mix, m_norm_ffn, m_norm_final, m_m_w_in, m_m_b_if, m_m_head_norm, m_m_w_out, m_r_w_in, m_r_conv_w, m_r_conv_b, m_r_gate_w, m_r_gate_b, m_r_a_param, m_r_w_out, m_ffn_w_in, m_ffn_w_out, v_norm_mix, v_norm_ffn, v_norm_final, v_m_w_in, v_m_b_if, v_m_head_norm, v_m_w_out, v_r_w_in, v_r_conv_w, v_r_conv_b, v_r_gate_w, v_r_gate_b, v_r_a_param, v_r_w_out, v_ffn_w_in, v_ffn_w_out):
    names = ["norm_mix", "norm_ffn", "norm_final", "m_w_in", "m_b_if", "m_head_norm", "m_w_out", "r_w_in", "r_conv_w",
             "r_conv_b", "r_gate_w", "r_gate_b", "r_a_param", "r_w_out", "ffn_w_in", "ffn_w_out"]
    w = dict(zip(names, [norm_mix, norm_ffn, norm_final, m_w_in, m_b_if, m_head_norm, m_w_out, r_w_in, r_conv_w,
                         r_conv_b, r_gate_w, r_gate_b, r_a_param, r_w_out, ffn_w_in, ffn_w_out]))
    mom = dict(zip(names, [m_norm_mix, m_norm_ffn, m_norm_final, m_m_w_in, m_m_b_if, m_m_head_norm, m_m_w_out, m_r_w_in,
                           m_r_conv_w, m_r_conv_b, m_r_gate_w, m_r_gate_b, m_r_a_param, m_r_w_out, m_ffn_w_in, m_ffn_w_out]))
    var = dict(zip(names, [v_norm_mix, v_norm_ffn, v_norm_final, v_m_w_in, v_m_b_if, v_m_head_norm, v_m_w_out, v_r_w_in,
                           v_r_conv_w, v_r_conv_b, v_r_gate_w, v_r_gate_b, v_r_a_param, v_r_w_out, v_ffn_w_in, v_ffn_w_out]))
    D = x.shape[-1]
    W = D // RBLOCKS
    nproj = m_w_in.shape[2] * N_CHIPS
    shard_col = lax.axis_index("x") * 2 + lax.axis_index("y")
    jc_idx = jnp.stack([shard_col, lax.axis_index("c")]).astype(jnp.int32)

    def cols(g, lead):
        nl = len(lead)
        g = g.reshape((N_CHIPS,) + lead + (-1,))
        g = jnp.moveaxis(g, 0, nl)
        return g.reshape(lead + (-1,))

    def uncols(a, lead):
        nl = len(lead)
        a = a.reshape(lead + (N_CHIPS, -1))
        a = jnp.moveaxis(a, nl, 0)
        return a.reshape(N_CHIPS, math.prod(lead), -1)

    shard = {"m_w_in": m_w_in[0], "m_w_out": m_w_out[0], "r_w_in": r_w_in[0],
             "gate_w": r_gate_w[0].reshape(-1, r_gate_w.shape[-1]), "r_w_out": r_w_out[0]}
    kind = {"m_w_in": "stack", "m_w_out": "row", "r_w_in": "col", "gate_w": "stack", "r_w_out": "row"}
    for l in range(2):
        shard[f"ffn{l}_w_in"], kind[f"ffn{l}_w_in"] = ffn_w_in[l], "col"
        shard[f"ffn{l}_w_out"], kind[f"ffn{l}_w_out"] = ffn_w_out[l], "row"
    shard = {n: a.astype(BF16) for n, a in shard.items()}
    piece = {n: _Piece(kind[n], *shard[n].shape) for n in shard}
    small_sharded = ["r_conv_w", "r_conv_b", "r_gate_b", "r_a_param"]
    small_pack = _pack_rows([w[n] for n in small_sharded], mult=16)
    post = {"m_w_in": lambda a: jnp.pad(cols(a, (D,)), ((0, 0), (0, 3 * D + IF_PAD - nproj))),
            "gate_w": lambda a: cols(a, (RBLOCKS, W))}

    def gather(names):
        return _Gather(names, [shard[n] for n in names], [piece[n] for n in names], post)

    first = _Gather(["m_w_in", "small"], [shard["m_w_in"], small_pack],
                    [piece["m_w_in"], _Piece("stack", *small_pack.shape)], post)
    got = dict(zip(first.names, _run_exchange("gather_first", first)))
    weights = {"m_w_in": post["m_w_in"](got["m_w_in"])}
    gather_plan = {"m_proj": gather(["ffn0_w_in"]),
                   "mlstm_fwd": gather(["m_w_out", "ffn0_w_out"]),
                   "ffn0_in": gather(["r_w_in", "gate_w", "r_w_out", "ffn1_w_out"]),
                   "rglru_fwd": gather(["ffn1_w_in"])}
    sp = got["small"]
    ws = D // N_CHIPS
    wg = 2 * W // N_CHIPS
    r1, r2, r3 = CONV_W * ws // LANES, (CONV_W + 1) * ws // LANES, ((CONV_W + 1) * ws + RBLOCKS * wg) // LANES
    conv_w_f = sp[:, :r1].reshape(N_CHIPS, CONV_W, ws).transpose(1, 0, 2).reshape(CONV_W, D)
    conv_b_f = sp[:, r1:r2].reshape(1, D)
    gate_b_f = sp[:, r2:r3].reshape(N_CHIPS, RBLOCKS, wg).transpose(1, 0, 2).reshape(RBLOCKS, 1, 2 * W)
    a_param_f = sp[:, r3:r3 + ws // LANES].reshape(1, D)
    p = dict(
        norm_mix=norm_mix, norm_ffn=norm_ffn, norm_final=norm_final.reshape(1, D),
        m_b_if=jnp.pad(m_b_if, ((0, 0), (0, IF_PAD - m_b_if.shape[1]))), m_head_norm=m_head_norm,
        conv_w=conv_w_f, conv_b=conv_b_f, gate_b=gate_b_f, a_param=a_param_f,
    )

    gsrc = {}

    def scatter(names):
        def make(g):
            for n in names:
                if n == "m_w_in":
                    gsrc[n] = uncols(g[n][:, :nproj], (D,))
                elif n == "gate_w":
                    gsrc[n] = uncols(g[n], (RBLOCKS, W))
                else:
                    gsrc[n] = g[n]
            return _Scatter(names, [gsrc[n] for n in names], [piece[n] for n in names])
        return make

    scatter_plan = {"ffn1_dwin": scatter(["ffn1_w_out"]), "ffn1_dx": scatter(["ffn1_w_in"]),
                    "r_dx": scatter(["r_w_out", "gate_w"]), "ffn0_dact": scatter(["r_w_in"]),
                    "ffn0_dwin": scatter(["ffn0_w_out"]), "ffn0_dx": scatter(["ffn0_w_in"]),
                    "m_dwin": scatter(["m_w_out"]), "m_dx": scatter(["m_w_in"])}
    loss_part, grad_x, g, landed = _step(x[0], loss_target[0], p, weights, gather_plan, scatter_plan)

    big = list(shard)
    halves = [_rs_sum(f"rs_sum_{n}", piece[n], gsrc[n], landed[n], jc_idx) for n in big]
    reduced = {n: r.reshape(shard[n].shape) for n, r in zip(big, _rs_share_halves(halves))}

    small_all = ["norm_mix", "norm_ffn", "norm_final", "m_b_if", "m_head_norm", "r_conv_w", "r_conv_b", "r_gate_b", "r_a_param"]
    gsmall = [g["norm_mix"], g["norm_ffn"], g["norm_final"], g["m_b_if"], g["m_head_norm"], g["conv_w"], g["conv_b"],
              g["gate_b"].reshape(RBLOCKS, 2 * W), g["a_param"], loss_part]
    summed = _unpack_rows(_allreduce_small(_pack_rows(gsmall)), [a.shape for a in gsmall])
    loss = summed[-1][0, 0]
    gs = dict(zip(small_all, summed[:-1]))
    gs["norm_final"] = gs["norm_final"].reshape(D)
    gs["m_b_if"] = gs["m_b_if"][:, :m_b_if.shape[1]]
    gs["r_conv_w"] = lax.dynamic_slice_in_dim(gs["r_conv_w"], shard_col * ws, ws, axis=1).reshape(r_conv_w.shape)
    gs["r_conv_b"] = lax.dynamic_slice_in_dim(gs["r_conv_b"], shard_col * ws, ws, axis=1).reshape(r_conv_b.shape)
    gs["r_gate_b"] = lax.dynamic_slice_in_dim(gs["r_gate_b"], shard_col * (2 * W // N_CHIPS), 2 * W // N_CHIPS,
                                              axis=1).reshape(r_gate_b.shape)
    gs["r_a_param"] = lax.dynamic_slice_in_dim(gs["r_a_param"], shard_col * ws, ws, axis=1).reshape(r_a_param.shape)

    grads, delta, new_m, new_v = {}, {}, {}, {}
    layers = {"m_w_in": ["m_w_in"], "m_w_out": ["m_w_out"], "r_w_in": ["r_w_in"], "r_gate_w": ["gate_w"],
              "r_w_out": ["r_w_out"], "ffn_w_in": ["ffn0_w_in", "ffn1_w_in"], "ffn_w_out": ["ffn0_w_out", "ffn1_w_out"]}
    for n, parts in layers.items():
        shp = w[n].shape
        if n == "m_w_in":
            tr_ = lambda a: jnp.swapaxes(a, -1, -2)
            res = _adamw(f"adamw_{n}", tr_(w[n]), [tr_(reduced[n])], tr_(mom[n]), tr_(var[n]))
            grads[n], delta[n], new_m[n], new_v[n] = (tr_(a) for a in res)
            continue
        as3d = lambda a: a.reshape((len(parts),) + shard[parts[0]].shape)
        res = _adamw(f"adamw_{n}", as3d(w[n]), [reduced[q] for q in parts], as3d(mom[n]), as3d(var[n]))
        grads[n], delta[n], new_m[n], new_v[n] = (a.reshape(shp) for a in res)

    packs = [_pack_rows([t[n] for n in small_all])[None] for t in (w, gs, mom, var)]
    outs = _adamw("adamw_small", packs[0], [packs[1][0]], packs[2], packs[3])
    for dst, pk in zip((delta, new_m, new_v), outs[1:]):
        for n, a in zip(small_all, _unpack_rows(pk[0], [w[n].shape for n in small_all])):
            dst[n] = a
    for n in small_all:
        grads[n] = gs[n]
    return (loss, grad_x[None], *[grads[n] for n in names], *[delta[n] for n in names],
            *[new_m[n] for n in names], *[new_v[n] for n in names])
```
